```python
import math
import jax, jax.numpy as jnp
from jax import lax
import numpy as np

D_MODEL = 4096
BATCH = 4
SEQ = 2048
DEPTH = 2

N_MIXERS = 2
N_MEM = 256
TOKEN_WIDTH = 3 * D_MODEL // 4
MEM_WIDTH = D_MODEL // 4
MEM_HEADS = 4
MEM_HEAD_DIM = MEM_WIDTH // MEM_HEADS
RET_HEAD_DIM = 256
RET_HEADS = TOKEN_WIDTH // RET_HEAD_DIM
RET_CHUNK = 128
SWA_HEAD_DIM = 64
SWA_HEADS = TOKEN_WIDTH // SWA_HEAD_DIM
SWA_GROUP = 8
SWA_KV_HEADS = SWA_HEADS // SWA_GROUP
WINDOW = 128
D_FF = ((8 * D_MODEL + 3 * 256 - 1) // (3 * 256)) * 256
ROPE_THETA = 10000.0
EPS = 1e-6
RET_IN_COLS = 4 * TOKEN_WIDTH + MEM_WIDTH
SWA_IN_COLS = TOKEN_WIDTH + 2 * SWA_KV_HEADS * SWA_HEAD_DIM + MEM_WIDTH

kernel_name = "hybrid_retention_swa_sink_memory_block"


def rmsnorm(x, g):
    xf = x.astype(jnp.float32)
    y = xf * lax.rsqrt(jnp.mean(xf * xf, axis=-1, keepdims=True) + EPS)
    return (y * g.astype(jnp.float32)).astype(x.dtype)


def rope(x, positions):
    d = x.shape[-1]
    inv_freq = ROPE_THETA ** (-jnp.arange(0, d, 2, dtype=jnp.float32) / d)
    ang = positions.astype(jnp.float32)[..., None] * inv_freq
    cos = jnp.cos(ang)[:, :, None, :]
    sin = jnp.sin(ang)[:, :, None, :]
    xf = x.astype(jnp.float32)
    x1, x2 = xf[..., : d // 2], xf[..., d // 2:]
    return jnp.concatenate([x1 * cos - x2 * sin, x2 * cos + x1 * sin], axis=-1).astype(x.dtype)


def chunkwise_retention(q, k, v):
    B, S, H, D = q.shape
    C = RET_CHUNK
    N = S // C
    q, k, v = (t.astype(jnp.float32).reshape(B, N, C, H, D) for t in (q, k, v))
    log_g = jnp.log1p(-jnp.exp2(-5.0 - jnp.arange(H, dtype=jnp.float32)))
    idx = jnp.arange(C, dtype=jnp.float32)
    diff = idx[:, None] - idx[None, :]
    intra = jnp.where(diff[None] >= 0, jnp.exp(jnp.maximum(diff, 0.0)[None] * log_g[:, None, None]), 0.0)
    scores = jnp.einsum('bnihd,bnjhd->bnhij', q, k) * intra
    o_intra = jnp.einsum('bnhij,bnjhe->bnihe', scores, v)
    q_dec = jnp.exp((idx + 1.0)[:, None] * log_g[None, :])
    k_dec = jnp.exp((C - 1.0 - idx)[:, None] * log_g[None, :])
    chunk_dec = jnp.exp(C * log_g)

    def step(state, inp):
        qc, kc, vc = inp
        cross = jnp.einsum('bihd,bhde->bihe', qc * q_dec[None, :, :, None], state)
        state = state * chunk_dec[None, :, None, None] + jnp.einsum(
            'bjhd,bjhe->bhde', kc * k_dec[None, :, :, None], vc)
        return state, cross

    xs = tuple(jnp.moveaxis(t, 1, 0) for t in (q, k, v))
    _, cross = lax.scan(step, jnp.zeros((B, H, D, D), jnp.float32), xs)
    o = o_intra + jnp.moveaxis(cross, 0, 1)
    return o.reshape(B, S, H, D)


def retention_mixer(cols, positions, ret_norm_g):
    B, S, _ = cols.shape
    q, k, v, g = jnp.split(cols, 4, axis=-1)
    q = rope(q.reshape(B, S, RET_HEADS, RET_HEAD_DIM), positions)
    k = rope(k.reshape(B, S, RET_HEADS, RET_HEAD_DIM), positions) * (RET_HEAD_DIM ** -0.5)
    v = v.reshape(B, S, RET_HEADS, RET_HEAD_DIM)
    o = chunkwise_retention(q, k, v)
    o = rmsnorm(o, ret_norm_g.reshape(RET_HEADS, RET_HEAD_DIM))
    return jax.nn.silu(g) * o.reshape(B, S, TOKEN_WIDTH).astype(cols.dtype)


def swa_sink_mixer(cols, positions, q_norm_g, k_norm_g, sinks):
    B, S, _ = cols.shape
    Dh, KV, G, W = SWA_HEAD_DIM, SWA_KV_HEADS, SWA_GROUP, WINDOW
    kv_w = KV * Dh
    q = cols[..., :TOKEN_WIDTH]
    k = cols[..., TOKEN_WIDTH:TOKEN_WIDTH + kv_w]
    v = cols[..., TOKEN_WIDTH + kv_w:]
    q = rope(rmsnorm(q.reshape(B, S, SWA_HEADS, Dh), q_norm_g), positions)
    k = rope(rmsnorm(k.reshape(B, S, KV, Dh), k_norm_g), positions)
    v = v.reshape(B, S, KV, Dh)
    NB = S // W
    qb = q.reshape(B, NB, W, KV, G, Dh)

    def with_prev(t):
        tb = t.reshape(B, NB, W, KV, Dh)
        prev = jnp.pad(tb[:, :-1], ((0, 0), (1, 0), (0, 0), (0, 0), (0, 0)))
        return jnp.concatenate([prev, tb], axis=2)

    kb, vb = with_prev(k), with_prev(v)
    scores = jnp.einsum('bnqkgd,bnskd->bnkgqs', qb, kb).astype(jnp.float32) * (Dh ** -0.5)
    i = jnp.arange(W)[:, None]
    j = jnp.arange(2 * W)[None, :]
    band = (j > i) & (j <= i + W)
    valid = band[None] & ((jnp.arange(NB)[:, None, None] > 0) | (j >= W)[None])
    scores = jnp.where(valid[None, :, None, None], scores, jnp.finfo(jnp.float32).min)
    sink = jnp.broadcast_to(sinks.astype(jnp.float32).reshape(1, 1, KV, G, 1, 1), scores.shape[:-1] + (1,))
    probs = jax.nn.softmax(jnp.concatenate([scores, sink], axis=-1), axis=-1)[..., :-1]
    out = jnp.einsum('bnkgqs,bnskd->bnqkgd', probs.astype(vb.dtype), vb)
    return out.reshape(B, S, TOKEN_WIDTH)


def memory_cross_attention(mq, mk, mv, q_norm_g):
    B, S, _ = mq.shape
    q = rmsnorm(mq.reshape(B, S, MEM_HEADS, MEM_HEAD_DIM), q_norm_g)
    s = jnp.einsum('bshd,bmhd->bhsm', q, mk).astype(jnp.float32) * (MEM_HEAD_DIM ** -0.5)
    p = jax.nn.softmax(s, axis=-1)
    return jnp.einsum('bhsm,bmhd->bshd', p.astype(mv.dtype), mv).reshape(B, S, MEM_WIDTH)


def swiglu(h, w_gate_up, w_down):
    gate, up = jnp.split(h @ w_gate_up, 2, axis=-1)
    return (jax.nn.silu(gate) * up) @ w_down


def hybrid_layer(x, positions, mk, mv, mixer_fn, mixer_params, attn_norm_g, w_in,
                 mem_q_norm_g, w_o, ffn_norm_g, w_gate_up, w_down):
    h = rmsnorm(x, attn_norm_g)
    proj = h @ w_in
    tok_out = mixer_fn(proj[..., :-MEM_WIDTH], positions, *mixer_params)
    mem_out = memory_cross_attention(proj[..., -MEM_WIDTH:], mk, mv, mem_q_norm_g)
    x = x + jnp.concatenate([tok_out, mem_out], axis=-1) @ w_o
    return x + swiglu(rmsnorm(x, ffn_norm_g), w_gate_up, w_down)


def setup_inputs(seed: int = 0) -> dict:
    key = jax.random.key(seed)
    ks = iter(jax.random.split(key, 32))
    f32 = jnp.float32

    def w(shape, fan_in):
        return jax.random.normal(next(ks), shape, f32) * (fan_in ** -0.5)

    def gain(n):
        return 1.0 + 0.02 * jax.random.normal(next(ks), (n,), f32)

    x = jax.random.normal(next(ks), (BATCH, SEQ, D_MODEL), f32)
    mem = jax.random.normal(next(ks), (BATCH, N_MEM, D_MODEL), f32)
    offsets = jax.random.randint(next(ks), (BATCH, 1), 0, 1024, dtype=jnp.int32)
    positions = (offsets + jnp.arange(SEQ, dtype=jnp.int32)[None, :]).astype(jnp.int32)
    return {
        "x": x,
        "mem": mem,
        "positions": positions,
        "mem_norm_g": gain(D_MODEL),
        "w_mem_kv": w((D_MODEL, 2 * MEM_WIDTH), D_MODEL),
        "mem_k_norm_g": gain(MEM_HEAD_DIM),
        "l0_attn_norm_g": gain(D_MODEL),
        "l0_w_in": w((D_MODEL, RET_IN_COLS), D_MODEL),
        "l0_ret_norm_g": gain(TOKEN_WIDTH),
        "l0_mem_q_norm_g": gain(MEM_HEAD_DIM),
        "l0_w_o": w((D_MODEL, D_MODEL), D_MODEL),
        "l0_ffn_norm_g": gain(D_MODEL),
        "l0_w_gate_up": w((D_MODEL, 2 * D_FF), D_MODEL),
        "l0_w_down": w((D_FF, D_MODEL), D_FF),
        "l1_attn_norm_g": gain(D_MODEL),
        "l1_w_in": w((D_MODEL, SWA_IN_COLS), D_MODEL),
        "l1_q_norm_g": gain(SWA_HEAD_DIM),
        "l1_k_norm_g": gain(SWA_HEAD_DIM),
        "l1_sinks": 0.5 * jax.random.normal(next(ks), (SWA_HEADS,), f32),
        "l1_mem_q_norm_g": gain(MEM_HEAD_DIM),
        "l1_w_o": w((D_MODEL, D_MODEL), D_MODEL),
        "l1_ffn_norm_g": gain(D_MODEL),
        "l1_w_gate_up": w((D_MODEL, 2 * D_FF), D_MODEL),
        "l1_w_down": w((D_FF, D_MODEL), D_FF),
    }


def reference(x, mem, positions, mem_norm_g, w_mem_kv, mem_k_norm_g,
              l0_attn_norm_g, l0_w_in, l0_ret_norm_g, l0_mem_q_norm_g, l0_w_o,
              l0_ffn_norm_g, l0_w_gate_up, l0_w_down,
              l1_attn_norm_g, l1_w_in, l1_q_norm_g, l1_k_norm_g, l1_sinks, l1_mem_q_norm_g,
              l1_w_o, l1_ffn_norm_g, l1_w_gate_up, l1_w_down):
    B = x.shape[0]
    mkv = rmsnorm(mem, mem_norm_g) @ w_mem_kv
    mk, mv = jnp.split(mkv, 2, axis=-1)
    mk = rmsnorm(mk.reshape(B, N_MEM, MEM_HEADS, MEM_HEAD_DIM), mem_k_norm_g)
    mv = mv.reshape(B, N_MEM, MEM_HEADS, MEM_HEAD_DIM)

    mixers = (retention_mixer, swa_sink_mixer)
    layer_params = [
        ((l0_ret_norm_g,), l0_attn_norm_g, l0_w_in, l0_mem_q_norm_g, l0_w_o,
         l0_ffn_norm_g, l0_w_gate_up, l0_w_down),
        ((l1_q_norm_g, l1_k_norm_g, l1_sinks), l1_attn_norm_g, l1_w_in, l1_mem_q_norm_g, l1_w_o,
         l1_ffn_norm_g, l1_w_gate_up, l1_w_down),
    ]
    for i in range(DEPTH):
        mixer_params, attn_g, w_in, mq_g, w_o, ffn_g, w_gu, w_dn = layer_params[i]
        x = hybrid_layer(x, positions, mk, mv, mixers[i % N_MIXERS], mixer_params,
                         attn_g, w_in, mq_g, w_o, ffn_g, w_gu, w_dn)
    return x
```

```python
import functools
import math

import jax
import jax.numpy as jnp
from jax import lax
from jax.experimental import pallas as pl
from jax.experimental.pallas import tpu as pltpu

D_MODEL = 4096
N_MEM = 256
TOKEN_WIDTH = 3 * D_MODEL // 4
MEM_WIDTH = D_MODEL // 4
MEM_HEADS = 4
MEM_HEAD_DIM = MEM_WIDTH // MEM_HEADS
RET_HEAD_DIM = 256
RET_HEADS = TOKEN_WIDTH // RET_HEAD_DIM
RET_CHUNK = 128
SWA_HEAD_DIM = 64
SWA_HEADS = TOKEN_WIDTH // SWA_HEAD_DIM
SWA_GROUP = 8
SWA_KV_HEADS = SWA_HEADS // SWA_GROUP
WINDOW = 128
D_FF = ((8 * D_MODEL + 3 * 256 - 1) // (3 * 256)) * 256
ROPE_THETA = 10000.0
EPS = 1e-6

LANES = 128
VMEM_LIMIT = 56 * 1024 * 1024

BF16 = jnp.bfloat16
F32 = jnp.float32


def _params(semantics, vmem=VMEM_LIMIT):
    return pltpu.CompilerParams(dimension_semantics=semantics, vmem_limit_bytes=vmem)


def _dot(a, b):
    return jnp.dot(a, b, preferred_element_type=F32)


def _dot_nt(a, b):
    return lax.dot_general(a, b, (((1,), (1,)), ((), ())), preferred_element_type=F32)


def _dot_tn(a, b):
    return lax.dot_general(a, b, (((0,), (0,)), ((), ())), preferred_element_type=F32)


def _silu(x):
    return x / (1.0 + jnp.exp(-x))


def _rmsnorm_kernel(x_ref, g_ref, o_ref):
    x = x_ref[...]
    ms = jnp.mean(x * x, axis=-1, keepdims=True)
    o_ref[...] = (x * lax.rsqrt(ms + EPS) * g_ref[...]).astype(o_ref.dtype)


def rmsnorm_rows(x, g, block_rows=512):
    t, d = x.shape
    return pl.pallas_call(
        _rmsnorm_kernel,
        grid=(t // block_rows,),
        in_specs=[pl.BlockSpec((block_rows, d), lambda i: (i, 0)),
                  pl.BlockSpec((1, d), lambda i: (0, 0))],
        out_specs=pl.BlockSpec((block_rows, d), lambda i: (i, 0)),
        out_shape=jax.ShapeDtypeStruct((t, d), BF16),
        compiler_params=_params(("parallel",)),
        name="rmsnorm",
    )(x, g.reshape(1, d))


def _matmul_kernel(x_ref, w_ref, o_ref):
    o_ref[...] = _dot(x_ref[...], w_ref[...]).astype(o_ref.dtype)


def _pick_block(n, candidates):
    for c in candidates:
        if n % c == 0:
            return c
    raise ValueError(f"no block size for {n}")


def matmul(x, w, out_dtype):
    m, k = x.shape
    n = w.shape[1]
    bm = _pick_block(m, (1024, 512, 256))
    bn = _pick_block(n, (1024, 512, 256))
    return pl.pallas_call(
        _matmul_kernel,
        grid=(m // bm, n // bn),
        in_specs=[pl.BlockSpec((bm, k), lambda i, j: (i, 0)),
                  pl.BlockSpec((k, bn), lambda i, j: (0, j))],
        out_specs=pl.BlockSpec((bm, bn), lambda i, j: (i, j)),
        out_shape=jax.ShapeDtypeStruct((m, n), out_dtype),
        compiler_params=_params(("parallel", "arbitrary")),
        name="matmul",
    )(x, w)


def _rope_table_kernel(pos_ref, freq_ref, sign_ref, cos_ref, sin_ref):
    ang = pos_ref[...] * freq_ref[...]
    cos_ref[...] = jnp.cos(ang)
    sin_ref[...] = jnp.sin(ang) * sign_ref[...]


def rope_tables(pos_f, freq, sign):
    b, s, _ = pos_f.shape
    out = jax.ShapeDtypeStruct((b, s, LANES), F32)
    return pl.pallas_call(
        _rope_table_kernel,
        grid=(b,),
        in_specs=[pl.BlockSpec((None, s, 1), lambda i: (i, 0, 0)),
                  pl.BlockSpec((1, LANES), lambda i: (0, 0)),
                  pl.BlockSpec((1, LANES), lambda i: (0, 0))],
        out_specs=[pl.BlockSpec((None, s, LANES), lambda i: (i, 0, 0))] * 2,
        out_shape=[out, out],
        compiler_params=_params(("parallel",)),
        name="rope_tables",
    )(pos_f, freq, sign)


def _retention_kernel(lg_ref, q_ref, k_ref, v_ref, g_ref, cos_ref, sin_ref, gain_ref,
                      o_ref, state_ref, *, seq):
    h = pl.program_id(1)
    lg = lg_ref[h]
    c = RET_CHUNK
    half = RET_HEAD_DIM // 2
    row = lax.broadcasted_iota(jnp.int32, (c, c), 0)
    col = lax.broadcasted_iota(jnp.int32, (c, c), 1)
    diff = (row - col).astype(F32)
    intra = jnp.where(diff >= 0, jnp.exp(jnp.maximum(diff, 0.0) * lg), 0.0)
    idx = lax.broadcasted_iota(jnp.int32, (c, 1), 0).astype(F32)
    q_dec = jnp.exp((idx + 1.0) * lg)
    k_dec = jnp.exp((c - 1.0 - idx) * lg)
    chunk_dec = jnp.exp(jnp.full((1, 1), float(c), F32) * lg)
    gain = gain_ref[...]
    state_ref[...] = jnp.zeros_like(state_ref)

    def rope(x, cos, sin):
        x1, x2 = x[:, :half], x[:, half:]
        return jnp.concatenate([x1 * cos - x2 * sin, x2 * cos + x1 * sin], axis=1)

    def body(n, carry):
        rows = pl.ds(pl.multiple_of(n * c, c), c)
        cos = cos_ref[rows, :]
        sin = sin_ref[rows, :]
        qc = rope(q_ref[rows, :].astype(F32), cos, sin)
        kc = rope(k_ref[rows, :].astype(F32), cos, sin) * (RET_HEAD_DIM ** -0.5)
        vc = v_ref[rows, :]
        scores = _dot_nt(qc.astype(BF16), kc.astype(BF16)) * intra
        o = _dot(scores.astype(BF16), vc)
        state = state_ref[...]
        o = o + _dot((qc * q_dec).astype(BF16), state.astype(BF16))
        state_ref[...] = state * chunk_dec + _dot_tn((kc * k_dec).astype(BF16), vc)
        ms = jnp.mean(o * o, axis=-1, keepdims=True)
        y = o * lax.rsqrt(ms + EPS) * gain
        o_ref[rows, :] = (_silu(g_ref[rows, :].astype(F32)) * y).astype(o_ref.dtype)
        return carry

    lax.fori_loop(0, seq // c, body, 0)


def retention_mixer(proj, cos, sin, log_g, gain, batch, seq):
    hd = RET_HEAD_DIM
    nh = RET_HEADS

    def col_spec(offset):
        return pl.BlockSpec((seq, hd), lambda b, h, lg: (b, offset + h))

    table_spec = pl.BlockSpec((None, seq, LANES), lambda b, h, lg: (b, 0, 0))
    grid_spec = pltpu.PrefetchScalarGridSpec(
        num_scalar_prefetch=1,
        grid=(batch, nh),
        in_specs=[col_spec(0), col_spec(nh), col_spec(2 * nh), col_spec(3 * nh),
                  table_spec, table_spec,
                  pl.BlockSpec((None, 1, hd), lambda b, h, lg: (h, 0, 0))],
        out_specs=pl.BlockSpec((seq, hd), lambda b, h, lg: (b, h)),
        scratch_shapes=[pltpu.VMEM((hd, hd), F32)],
    )
    return pl.pallas_call(
        functools.partial(_retention_kernel, seq=seq),
        grid_spec=grid_spec,
        out_shape=jax.ShapeDtypeStruct((batch * seq, TOKEN_WIDTH), BF16),
        compiler_params=_params(("parallel", "parallel")),
        name="retention",
    )(log_g, proj, proj, proj, proj, cos, sin, gain.reshape(nh, 1, hd))


def _swa_kernel(sink_ref, q_ref, kc_ref, kp_ref, vc_ref, vp_ref, cosc_ref, sinc_ref,
                cosp_ref, sinp_ref, qg_ref, kg_ref, o_ref):
    n = pl.program_id(1)
    w = WINDOW
    dh = SWA_HEAD_DIM
    lane = lax.broadcasted_iota(jnp.int32, (1, LANES), 1)
    low_half = lane < dh
    first_rot = (lane % dh) < (dh // 2)
    r = lax.broadcasted_iota(jnp.int32, (LANES, LANES), 0) // dh
    cidx = lax.broadcasted_iota(jnp.int32, (LANES, LANES), 1) // dh
    ones_bd = (r == cidx).astype(BF16)

    def norm_rope(x, gain, cos, sin):
        ss = x * x
        hi = ss.astype(BF16)
        lo = (ss - hi.astype(F32)).astype(BF16)
        gs = _dot(hi, ones_bd) + _dot(lo, ones_bd)
        y = x * lax.rsqrt(gs * (1.0 / dh) + EPS) * gain
        rot = jnp.where(first_rot, pltpu.roll(y, LANES - dh // 2, 1), pltpu.roll(y, dh // 2, 1))
        return y * cos + rot * sin

    cos_c, sin_c = cosc_ref[...], sinc_ref[...]
    cos_p, sin_p = cosp_ref[...], sinp_ref[...]
    qg, kg = qg_ref[...], kg_ref[...]

    i_idx = lax.broadcasted_iota(jnp.int32, (w, 2 * w), 0)
    j_idx = lax.broadcasted_iota(jnp.int32, (w, 2 * w), 1)
    valid = (j_idx > i_idx) & (j_idx <= i_idx + w) & ((n > 0) | (j_idx >= w))
    valid = jnp.concatenate([valid] * SWA_GROUP, axis=0)
    neg = jnp.finfo(F32).min
    scale = dh ** -0.5

    n_kv_slabs = SWA_KV_HEADS * dh // LANES
    k_slabs, v_slabs = [], []
    for s in range(n_kv_slabs):
        cols = slice(s * LANES, (s + 1) * LANES)
        k_cur = norm_rope(kc_ref[:, cols].astype(F32), kg, cos_c, sin_c)
        k_prev = norm_rope(kp_ref[:, cols].astype(F32), kg, cos_p, sin_p)
        k_slabs.append(jnp.concatenate([k_prev, k_cur], axis=0))
        v_slabs.append(jnp.concatenate([vp_ref[:, cols], vc_ref[:, cols]], axis=0))

    for kv in range(SWA_KV_HEADS):
        k_slab = k_slabs[kv // 2]
        v_slab = v_slabs[kv // 2]
        if kv % 2 == 0:
            k_own = jnp.where(low_half, k_slab, 0.0)
            v_lo = jnp.where(low_half, v_slab, jnp.zeros_like(v_slab))
            v_hi = pltpu.roll(v_lo.astype(F32), dh, 1).astype(BF16)
        else:
            k_own = jnp.where(low_half, 0.0, k_slab)
            v_hi = jnp.where(low_half, jnp.zeros_like(v_slab), v_slab)
            v_lo = pltpu.roll(v_hi.astype(F32), dh, 1).astype(BF16)
        kk = (k_own + pltpu.roll(k_own, dh, 1)).astype(BF16)
        v_pair = jnp.concatenate([v_lo, v_hi], axis=0)

        q_rows, sink_rows = [], []
        for pair in range(SWA_GROUP // 2):
            col0 = (kv * SWA_GROUP // 2 + pair) * LANES
            qn = norm_rope(q_ref[:, col0:col0 + LANES].astype(F32), qg, cos_c, sin_c)
            q_rows.append(jnp.where(low_half, qn, 0.0))
            q_rows.append(jnp.where(low_half, 0.0, qn))
            for g in range(2):
                sink = sink_ref[kv * SWA_GROUP + 2 * pair + g]
                sink_rows.append(jnp.full((w, 1), sink, F32))
        qs = jnp.concatenate(q_rows, axis=0).astype(BF16)
        sinks = jnp.concatenate(sink_rows, axis=0)

        s = _dot_nt(qs, kk) * scale
        s = jnp.where(valid, s, neg)
        m = jnp.maximum(jnp.max(s, axis=-1, keepdims=True), sinks)
        p = jnp.exp(s - m)
        denom = jnp.sum(p, axis=-1, keepdims=True) + jnp.exp(sinks - m)
        p = (p / denom).astype(BF16)
        for pair in range(SWA_GROUP // 2):
            p_lo = p[(2 * pair) * w:(2 * pair + 1) * w, :]
            p_hi = p[(2 * pair + 1) * w:(2 * pair + 2) * w, :]
            o_pair = _dot(jnp.concatenate([p_lo, p_hi], axis=1), v_pair)
            col0 = (kv * SWA_GROUP // 2 + pair) * LANES
            o_ref[:, col0:col0 + LANES] = o_pair.astype(o_ref.dtype)


def swa_mixer(proj, cos, sin, q_gain, k_gain, sinks, batch, seq):
    w = WINDOW
    nb = seq // w
    kv_w = SWA_KV_HEADS * SWA_HEAD_DIM
    k_blk = TOKEN_WIDTH // kv_w

    def cur(b, n, s):
        return b * nb + n

    def prev(b, n, s):
        return b * nb + jnp.maximum(n - 1, 0)

    tab_cur = pl.BlockSpec((None, w, LANES), lambda b, n, s: (b, n, 0))
    tab_prev = pl.BlockSpec((None, w, LANES), lambda b, n, s: (b, jnp.maximum(n - 1, 0), 0))
    gain_spec = pl.BlockSpec((1, LANES), lambda b, n, s: (0, 0))
    grid_spec = pltpu.PrefetchScalarGridSpec(
        num_scalar_prefetch=1,
        grid=(batch, nb),
        in_specs=[pl.BlockSpec((w, TOKEN_WIDTH), lambda b, n, s: (cur(b, n, s), 0)),
                  pl.BlockSpec((w, kv_w), lambda b, n, s: (cur(b, n, s), k_blk)),
                  pl.BlockSpec((w, kv_w), lambda b, n, s: (prev(b, n, s), k_blk)),
                  pl.BlockSpec((w, kv_w), lambda b, n, s: (cur(b, n, s), k_blk + 1)),
                  pl.BlockSpec((w, kv_w), lambda b, n, s: (prev(b, n, s), k_blk + 1)),
                  tab_cur, tab_cur, tab_prev, tab_prev, gain_spec, gain_spec],
        out_specs=pl.BlockSpec((w, TOKEN_WIDTH), lambda b, n, s: (cur(b, n, s), 0)),
    )
    tile = LANES // SWA_HEAD_DIM
    return pl.pallas_call(
        _swa_kernel,
        grid_spec=grid_spec,
        out_shape=jax.ShapeDtypeStruct((batch * seq, TOKEN_WIDTH), BF16),
        compiler_params=_params(("parallel", "parallel")),
        name="swa",
    )(sinks, proj, proj, proj, proj, proj, cos, sin, cos, sin,
      jnp.tile(q_gain, tile).reshape(1, LANES), jnp.tile(k_gain, tile).reshape(1, LANES))


def _mem_attn_kernel(q_ref, mk_ref, mv_ref, qg_ref, kg_ref, o_ref):
    def norm(x, g):
        ms = jnp.mean(x * x, axis=-1, keepdims=True)
        return (x * lax.rsqrt(ms + EPS) * g).astype(BF16)

    q = norm(q_ref[...].astype(F32), qg_ref[...])
    k = norm(mk_ref[...], kg_ref[...])
    s = _dot_nt(q, k) * (MEM_HEAD_DIM ** -0.5)
    m = jnp.max(s, axis=-1, keepdims=True)
    p = jnp.exp(s - m)
    p = (p / jnp.sum(p, axis=-1, keepdims=True)).astype(BF16)
    o_ref[...] = _dot(p, mv_ref[...].astype(BF16)).astype(o_ref.dtype)


def memory_attention(proj, q_col_block, mkv, q_gain, k_gain, batch, seq, block_rows=512):
    hd = MEM_HEAD_DIM
    nblk = seq // block_rows
    gain_spec = pl.BlockSpec((1, hd), lambda b, i, h: (0, 0))
    return pl.pallas_call(
        _mem_attn_kernel,
        grid=(batch, nblk, MEM_HEADS),
        in_specs=[pl.BlockSpec((block_rows, hd), lambda b, i, h: (b * nblk + i, q_col_block + h)),
                  pl.BlockSpec((N_MEM, hd), lambda b, i, h: (b, h)),
                  pl.BlockSpec((N_MEM, hd), lambda b, i, h: (b, MEM_HEADS + h)),
                  gain_spec, gain_spec],
        out_specs=pl.BlockSpec((block_rows, hd), lambda b, i, h: (b * nblk + i, h)),
        out_shape=jax.ShapeDtypeStruct((batch * seq, MEM_WIDTH), BF16),
        compiler_params=_params(("parallel", "parallel", "parallel")),
        name="mem_attn",
    )(proj, mkv, mkv, q_gain.reshape(1, hd), k_gain.reshape(1, hd))


def _out_proj_kernel(tok_ref, mem_ref, w_ref, x_ref, o_ref):
    acc = _dot(tok_ref[...], w_ref[:TOKEN_WIDTH, :])
    acc = acc + _dot(mem_ref[...], w_ref[TOKEN_WIDTH:, :])
    o_ref[...] = x_ref[...] + acc


def out_projection(tok, mem, w_o, x):
    t = x.shape[0]
    bm, bn = 1024, 512
    return pl.pallas_call(
        _out_proj_kernel,
        grid=(t // bm, D_MODEL // bn),
        in_specs=[pl.BlockSpec((bm, TOKEN_WIDTH), lambda i, j: (i, 0)),
                  pl.BlockSpec((bm, MEM_WIDTH), lambda i, j: (i, 0)),
                  pl.BlockSpec((D_MODEL, bn), lambda i, j: (0, j)),
                  pl.BlockSpec((bm, bn), lambda i, j: (i, j))],
        out_specs=pl.BlockSpec((bm, bn), lambda i, j: (i, j)),
        out_shape=jax.ShapeDtypeStruct((t, D_MODEL), F32),
        compiler_params=_params(("parallel", "arbitrary")),
        name="out_proj",
    )(tok, mem, w_o, x)


FFN_BM = 1024
FFN_FC = 256
FFN_NC = 512


def _ffn_kernel(xn_ref, wg_ref, wu_ref, wd_ref, x_ref, o_ref, acc_ref, *, n_f):
    f = pl.program_id(1)
    n_slab = D_MODEL // FFN_NC

    @pl.when(f < n_f)
    def _():
        xn = xn_ref[...]
        gate = _dot(xn, wg_ref[...])
        up = _dot(xn, wu_ref[...])
        hid = (_silu(gate) * up).astype(BF16)
        for c in range(n_slab):
            part = _dot(hid, wd_ref[:, c * FFN_NC:(c + 1) * FFN_NC])

            @pl.when(f == 0)
            def _():
                acc_ref[c] = part

            @pl.when(f > 0)
            def _():
                acc_ref[c] += part

    @pl.when(f >= n_f)
    def _():
        o_ref[...] = x_ref[...] + acc_ref[f - n_f]


def ffn(xn, w_gate_up, w_down, x):
    t = x.shape[0]
    n_f = D_FF // FFN_FC
    n_slab = D_MODEL // FFN_NC
    last = n_f - 1

    def fidx(f):
        return jnp.minimum(f, last)

    def oidx(f):
        return jnp.maximum(f - n_f, 0)

    return pl.pallas_call(
        functools.partial(_ffn_kernel, n_f=n_f),
        grid=(t // FFN_BM, n_f + n_slab),
        in_specs=[pl.BlockSpec((FFN_BM, D_MODEL), lambda i, f: (i, 0), pipeline_mode=pl.Buffered(1)),
                  pl.BlockSpec((D_MODEL, FFN_FC), lambda i, f: (0, fidx(f))),
                  pl.BlockSpec((D_MODEL, FFN_FC), lambda i, f: (0, n_f + fidx(f))),
                  pl.BlockSpec((FFN_FC, D_MODEL), lambda i, f: (fidx(f), 0)),
                  pl.BlockSpec((FFN_BM, FFN_NC), lambda i, f: (i, oidx(f)))],
        out_specs=pl.BlockSpec((FFN_BM, FFN_NC), lambda i, f: (i, oidx(f))),
        out_shape=jax.ShapeDtypeStruct((t, D_MODEL), F32),
        scratch_shapes=[pltpu.VMEM((n_slab, FFN_BM, FFN_NC), F32)],
        compiler_params=_params(("parallel", "arbitrary")),
        name="ffn",
    )(xn, w_gate_up, w_gate_up, w_down, x)


def _rope_consts(head_dim):
    half = head_dim // 2
    inv_freq = ROPE_THETA ** (-jnp.arange(0, head_dim, 2, dtype=F32) / head_dim)
    reps = LANES // half if half < LANES else 1
    freq = jnp.tile(inv_freq, reps)[:LANES]
    if half >= LANES:
        sign = jnp.ones((LANES,), F32)
    else:
        sign = jnp.tile(jnp.concatenate([-jnp.ones((half,), F32), jnp.ones((half,), F32)]), reps // 2)
    return freq.reshape(1, LANES), sign.reshape(1, LANES)


def kernel(x, mem, positions, mem_norm_g, w_mem_kv, mem_k_norm_g, l0_attn_norm_g, l0_w_in, l0_ret_norm_g, l0_mem_q_norm_g, l0_w_o, l0_ffn_norm_g, l0_w_gate_up, l0_w_down, l1_attn_norm_g, l1_w_in, l1_q_norm_g, l1_k_norm_g, l1_sinks, l1_mem_q_norm_g, l1_w_o, l1_ffn_norm_g, l1_w_gate_up, l1_w_down):
    batch, seq, d = x.shape
    t = batch * seq
    x2 = x.reshape(t, d)

    pos_f = positions.astype(F32).reshape(batch, seq, 1)
    cos_r, sin_r = rope_tables(pos_f, *_rope_consts(RET_HEAD_DIM))
    cos_s, sin_s = rope_tables(pos_f, *_rope_consts(SWA_HEAD_DIM))
    log_g = jnp.log1p(-jnp.exp2(-5.0 - jnp.arange(RET_HEADS, dtype=F32)))

    mem_n = rmsnorm_rows(mem.reshape(batch * N_MEM, d), mem_norm_g)
    mkv = matmul(mem_n, w_mem_kv.astype(BF16), F32)

    h = rmsnorm_rows(x2, l0_attn_norm_g)
    proj = matmul(h, l0_w_in.astype(BF16), BF16)
    tok = retention_mixer(proj, cos_r, sin_r, log_g, l0_ret_norm_g, batch, seq)
    mo = memory_attention(proj, 4 * TOKEN_WIDTH // MEM_HEAD_DIM, mkv, l0_mem_q_norm_g, mem_k_norm_g,
                          batch, seq)
    x2 = out_projection(tok, mo, l0_w_o.astype(BF16), x2)
    x2 = ffn(rmsnorm_rows(x2, l0_ffn_norm_g), l0_w_gate_up.astype(BF16), l0_w_down.astype(BF16), x2)

    h = rmsnorm_rows(x2, l1_attn_norm_g)
    proj = matmul(h, l1_w_in.astype(BF16), BF16)
    tok = swa_mixer(proj, cos_s, sin_s, l1_q_norm_g, l1_k_norm_g, l1_sinks, batch, seq)
    q_col = (TOKEN_WIDTH + 2 * SWA_KV_HEADS * SWA_HEAD_DIM) // MEM_HEAD_DIM
    mo = memory_attention(proj, q_col, mkv, l1_mem_q_norm_g, mem_k_norm_g, batch, seq)
    x2 = out_projection(tok, mo, l1_w_o.astype(BF16), x2)
    x2 = ffn(rmsnorm_rows(x2, l1_ffn_norm_g), l1_w_gate_up.astype(BF16), l1_w_down.astype(BF16), x2)
    return x2.reshape(batch, seq, d)
```

```python
import functools
import math

import jax
import jax.numpy as jnp
from jax import lax
from jax.experimental import pallas as pl
from jax.experimental.pallas import tpu as pltpu

D_MODEL = 4096
N_MEM = 256
TOKEN_WIDTH = 3 * D_MODEL // 4
MEM_WIDTH = D_MODEL // 4
MEM_HEADS = 4
MEM_HEAD_DIM = MEM_WIDTH // MEM_HEADS
RET_HEAD_DIM = 256
RET_HEADS = TOKEN_WIDTH // RET_HEAD_DIM
RET_CHUNK = 128
SWA_HEAD_DIM = 64
SWA_HEADS = TOKEN_WIDTH // SWA_HEAD_DIM
SWA_GROUP = 8
SWA_KV_HEADS = SWA_HEADS // SWA_GROUP
WINDOW = 128
D_FF = ((8 * D_MODEL + 3 * 256 - 1) // (3 * 256)) * 256
ROPE_THETA = 10000.0
EPS = 1e-6

LANES = 128
VMEM_LIMIT = 56 * 1024 * 1024

BF16 = jnp.bfloat16
F32 = jnp.float32


def _params(semantics, vmem=VMEM_LIMIT):
    return pltpu.CompilerParams(dimension_semantics=semantics, vmem_limit_bytes=vmem)


def _dot(a, b):
    return jnp.dot(a, b, preferred_element_type=F32)


def _dot_nt(a, b):
    return lax.dot_general(a, b, (((1,), (1,)), ((), ())), preferred_element_type=F32)


def _dot_tn(a, b):
    return lax.dot_general(a, b, (((0,), (0,)), ((), ())), preferred_element_type=F32)


def _silu(x):
    return x / (1.0 + jnp.exp(-x))


def _rmsnorm_kernel(x_ref, g_ref, o_ref):
    x = x_ref[...]
    ms = jnp.mean(x * x, axis=-1, keepdims=True)
    o_ref[...] = (x * lax.rsqrt(ms + EPS) * g_ref[...]).astype(o_ref.dtype)


def rmsnorm_rows(x, g, block_rows=512):
    t, d = x.shape
    return pl.pallas_call(
        _rmsnorm_kernel,
        grid=(t // block_rows,),
        in_specs=[pl.BlockSpec((block_rows, d), lambda i: (i, 0)),
                  pl.BlockSpec((1, d), lambda i: (0, 0))],
        out_specs=pl.BlockSpec((block_rows, d), lambda i: (i, 0)),
        out_shape=jax.ShapeDtypeStruct((t, d), BF16),
        compiler_params=_params(("parallel",)),
        name="rmsnorm",
    )(x, g.reshape(1, d))


def _matmul_kernel(x_ref, w_ref, o_ref):
    o_ref[...] = _dot(x_ref[...], w_ref[...].astype(BF16)).astype(o_ref.dtype)


def _pick_block(n, candidates):
    for c in candidates:
        if n % c == 0:
            return c
    raise ValueError(f"no block size for {n}")


def matmul(x, w, out_dtype):
    m, k = x.shape
    n = w.shape[1]
    bm = _pick_block(m, (2048, 1024))
    bn = _pick_block(n, (512, 256))
    return pl.pallas_call(
        _matmul_kernel,
        grid=(m // bm, n // bn),
        in_specs=[pl.BlockSpec((bm, k), lambda i, j: (i, 0), pipeline_mode=pl.Buffered(1)),
                  pl.BlockSpec((k, bn), lambda i, j: (0, j))],
        out_specs=pl.BlockSpec((bm, bn), lambda i, j: (i, j)),
        out_shape=jax.ShapeDtypeStruct((m, n), out_dtype),
        compiler_params=_params(("parallel", "arbitrary")),
        name="matmul",
    )(x, w)


def _rope_table_kernel(pos_ref, freq_ref, sign_ref, cos_ref, sin_ref):
    ang = pos_ref[...] * freq_ref[...]
    cos_ref[...] = jnp.cos(ang)
    sin_ref[...] = jnp.sin(ang) * sign_ref[...]


def rope_tables(pos_f, freq, sign):
    b, s, _ = pos_f.shape
    out = jax.ShapeDtypeStruct((b, s, LANES), F32)
    return pl.pallas_call(
        _rope_table_kernel,
        grid=(b,),
        in_specs=[pl.BlockSpec((None, s, 1), lambda i: (i, 0, 0)),
                  pl.BlockSpec((1, LANES), lambda i: (0, 0)),
                  pl.BlockSpec((1, LANES), lambda i: (0, 0))],
        out_specs=[pl.BlockSpec((None, s, LANES), lambda i: (i, 0, 0))] * 2,
        out_shape=[out, out],
        compiler_params=_params(("parallel",)),
        name="rope_tables",
    )(pos_f, freq, sign)


def _retention_kernel(lg_ref, q_ref, k_ref, v_ref, g_ref, cos_ref, sin_ref, gain_ref,
                      o_ref, state_ref, *, seq):
    h = pl.program_id(1)
    lg = lg_ref[h]
    c = RET_CHUNK
    half = RET_HEAD_DIM // 2
    row = lax.broadcasted_iota(jnp.int32, (c, c), 0)
    col = lax.broadcasted_iota(jnp.int32, (c, c), 1)
    diff = (row - col).astype(F32)
    intra = jnp.where(diff >= 0, jnp.exp(jnp.maximum(diff, 0.0) * lg), 0.0)
    idx = lax.broadcasted_iota(jnp.int32, (c, 1), 0).astype(F32)
    q_dec = jnp.exp((idx + 1.0) * lg)
    k_dec = jnp.exp((c - 1.0 - idx) * lg)
    chunk_dec = jnp.exp(jnp.full((1, 1), float(c), F32) * lg)
    gain = gain_ref[...]
    state_ref[...] = jnp.zeros_like(state_ref)

    def rope(x, cos, sin):
        x1, x2 = x[:, :half], x[:, half:]
        return jnp.concatenate([x1 * cos - x2 * sin, x2 * cos + x1 * sin], axis=1)

    def body(n, carry):
        rows = pl.ds(pl.multiple_of(n * c, c), c)
        cos = cos_ref[rows, :]
        sin = sin_ref[rows, :]
        qc = rope(q_ref[rows, :].astype(F32), cos, sin)
        kc = rope(k_ref[rows, :].astype(F32), cos, sin) * (RET_HEAD_DIM ** -0.5)
        vc = v_ref[rows, :]
        scores = _dot_nt(qc.astype(BF16), kc.astype(BF16)) * intra
        o = _dot(scores.astype(BF16), vc)
        state = state_ref[...]
        o = o + _dot((qc * q_dec).astype(BF16), state.astype(BF16))
        state_ref[...] = state * chunk_dec + _dot_tn((kc * k_dec).astype(BF16), vc)
        ms = jnp.mean(o * o, axis=-1, keepdims=True)
        y = o * lax.rsqrt(ms + EPS) * gain
        o_ref[rows, :] = (_silu(g_ref[rows, :].astype(F32)) * y).astype(o_ref.dtype)
        return carry

    lax.fori_loop(0, seq // c, body, 0)


def retention_mixer(proj, cos, sin, log_g, gain, batch, seq):
    hd = RET_HEAD_DIM
    nh = RET_HEADS

    def col_spec(offset):
        return pl.BlockSpec((seq, hd), lambda b, h, lg: (b, offset + h))

    table_spec = pl.BlockSpec((None, seq, LANES), lambda b, h, lg: (b, 0, 0))
    grid_spec = pltpu.PrefetchScalarGridSpec(
        num_scalar_prefetch=1,
        grid=(batch, nh),
        in_specs=[col_spec(0), col_spec(nh), col_spec(2 * nh), col_spec(3 * nh),
                  table_spec, table_spec,
                  pl.BlockSpec((None, 1, hd), lambda b, h, lg: (h, 0, 0))],
        out_specs=pl.BlockSpec((seq, hd), lambda b, h, lg: (b, h)),
        scratch_shapes=[pltpu.VMEM((hd, hd), F32)],
    )
    return pl.pallas_call(
        functools.partial(_retention_kernel, seq=seq),
        grid_spec=grid_spec,
        out_shape=jax.ShapeDtypeStruct((batch * seq, TOKEN_WIDTH), BF16),
        compiler_params=_params(("parallel", "parallel")),
        name="retention",
    )(log_g, proj, proj, proj, proj, cos, sin, gain.reshape(nh, 1, hd))


def _swa_kernel(sink_ref, q_ref, kc_ref, kp_ref, vc_ref, vp_ref, cosc_ref, sinc_ref,
                cosp_ref, sinp_ref, qg_ref, kg_ref, o_ref):
    n = pl.program_id(1)
    w = WINDOW
    dh = SWA_HEAD_DIM
    lane = lax.broadcasted_iota(jnp.int32, (1, LANES), 1)
    low_half = lane < dh
    first_rot = (lane % dh) < (dh // 2)
    r = lax.broadcasted_iota(jnp.int32, (LANES, LANES), 0) // dh
    cidx = lax.broadcasted_iota(jnp.int32, (LANES, LANES), 1) // dh
    ones_bd = (r == cidx).astype(BF16)

    def norm_rope(x, gain, cos, sin):
        ss = x * x
        hi = ss.astype(BF16)
        lo = (ss - hi.astype(F32)).astype(BF16)
        gs = _dot(hi, ones_bd) + _dot(lo, ones_bd)
        y = x * lax.rsqrt(gs * (1.0 / dh) + EPS) * gain
        rot = jnp.where(first_rot, pltpu.roll(y, LANES - dh // 2, 1), pltpu.roll(y, dh // 2, 1))
        return y * cos + rot * sin

    cos_c, sin_c = cosc_ref[...], sinc_ref[...]
    cos_p, sin_p = cosp_ref[...], sinp_ref[...]
    qg, kg = qg_ref[...], kg_ref[...]

    i_idx = lax.broadcasted_iota(jnp.int32, (w, 2 * w), 0)
    j_idx = lax.broadcasted_iota(jnp.int32, (w, 2 * w), 1)
    valid = (j_idx > i_idx) & (j_idx <= i_idx + w) & ((n > 0) | (j_idx >= w))
    valid = jnp.concatenate([valid, valid], axis=0)
    neg = jnp.finfo(F32).min
    scale = dh ** -0.5
    e_row = lax.broadcasted_iota(jnp.int32, (4 * w, LANES), 0) < 2 * w
    e_lane = lax.broadcasted_iota(jnp.int32, (4 * w, LANES), 1) < dh
    e_mat = (e_row == e_lane).astype(BF16)

    n_kv_slabs = SWA_KV_HEADS * dh // LANES
    k_slabs, v_slabs = [], []
    for s in range(n_kv_slabs):
        cols = slice(s * LANES, (s + 1) * LANES)
        k_cur = norm_rope(kc_ref[:, cols].astype(F32), kg, cos_c, sin_c)
        k_prev = norm_rope(kp_ref[:, cols].astype(F32), kg, cos_p, sin_p)
        k_slabs.append(jnp.concatenate([k_prev, k_cur], axis=0))
        v_slabs.append(jnp.concatenate([vp_ref[:, cols], vc_ref[:, cols]], axis=0))

    for kv in range(SWA_KV_HEADS):
        k_slab = k_slabs[kv // 2]
        v_slab = v_slabs[kv // 2]
        if kv % 2 == 0:
            k_own = jnp.where(low_half, k_slab, 0.0)
            v_lo = jnp.where(low_half, v_slab, jnp.zeros_like(v_slab))
            v_hi = pltpu.roll(v_lo.astype(F32), dh, 1).astype(BF16)
        else:
            k_own = jnp.where(low_half, 0.0, k_slab)
            v_hi = jnp.where(low_half, jnp.zeros_like(v_slab), v_slab)
            v_lo = pltpu.roll(v_hi.astype(F32), dh, 1).astype(BF16)
        kk = (k_own + pltpu.roll(k_own, dh, 1)).astype(BF16)
        v_ext = jnp.concatenate([jnp.concatenate([v_lo, v_hi], axis=0), e_mat], axis=1)

        for pair in range(SWA_GROUP // 2):
            col0 = (kv * SWA_GROUP // 2 + pair) * LANES
            qn = norm_rope(q_ref[:, col0:col0 + LANES].astype(F32), qg, cos_c, sin_c) * scale
            q2 = jnp.concatenate([jnp.where(low_half, qn, 0.0), jnp.where(low_half, 0.0, qn)],
                                 axis=0).astype(BF16)
            head = kv * SWA_GROUP + 2 * pair
            sinks = jnp.concatenate([jnp.full((w, 1), sink_ref[head], F32),
                                     jnp.full((w, 1), sink_ref[head + 1], F32)], axis=0)
            s = jnp.where(valid, _dot_nt(q2, kk), neg)
            m = jnp.maximum(jnp.max(s, axis=-1, keepdims=True), sinks)
            p = jnp.exp(s - m).astype(BF16)
            sink_p = jnp.exp(sinks - m)
            ov = _dot(jnp.concatenate([p[:w], p[w:]], axis=1), v_ext)
            denom = ov[:, LANES:] + jnp.where(low_half, sink_p[:w], sink_p[w:])
            o_ref[:, col0:col0 + LANES] = (ov[:, :LANES] / denom).astype(o_ref.dtype)


def swa_mixer(proj, cos, sin, q_gain, k_gain, sinks, batch, seq):
    w = WINDOW
    nb = seq // w
    kv_w = SWA_KV_HEADS * SWA_HEAD_DIM
    k_blk = TOKEN_WIDTH // kv_w

    def cur(b, n, s):
        return b * nb + n

    def prev(b, n, s):
        return b * nb + jnp.maximum(n - 1, 0)

    tab_cur = pl.BlockSpec((None, w, LANES), lambda b, n, s: (b, n, 0))
    tab_prev = pl.BlockSpec((None, w, LANES), lambda b, n, s: (b, jnp.maximum(n - 1, 0), 0))
    gain_spec = pl.BlockSpec((1, LANES), lambda b, n, s: (0, 0))
    grid_spec = pltpu.PrefetchScalarGridSpec(
        num_scalar_prefetch=1,
        grid=(batch, nb),
        in_specs=[pl.BlockSpec((w, TOKEN_WIDTH), lambda b, n, s: (cur(b, n, s), 0)),
                  pl.BlockSpec((w, kv_w), lambda b, n, s: (cur(b, n, s), k_blk)),
                  pl.BlockSpec((w, kv_w), lambda b, n, s: (prev(b, n, s), k_blk)),
                  pl.BlockSpec((w, kv_w), lambda b, n, s: (cur(b, n, s), k_blk + 1)),
                  pl.BlockSpec((w, kv_w), lambda b, n, s: (prev(b, n, s), k_blk + 1)),
                  tab_cur, tab_cur, tab_prev, tab_prev, gain_spec, gain_spec],
        out_specs=pl.BlockSpec((w, TOKEN_WIDTH), lambda b, n, s: (cur(b, n, s), 0)),
    )
    tile = LANES // SWA_HEAD_DIM
    return pl.pallas_call(
        _swa_kernel,
        grid_spec=grid_spec,
        out_shape=jax.ShapeDtypeStruct((batch * seq, TOKEN_WIDTH), BF16),
        compiler_params=_params(("parallel", "parallel")),
        name="swa",
    )(sinks, proj, proj, proj, proj, proj, cos, sin, cos, sin,
      jnp.tile(q_gain, tile).reshape(1, LANES), jnp.tile(k_gain, tile).reshape(1, LANES))


def _mem_attn_kernel(q_ref, mk_ref, mv_ref, qg_ref, kg_ref, o_ref):
    def norm(x, g):
        ms = jnp.mean(x * x, axis=-1, keepdims=True)
        return (x * lax.rsqrt(ms + EPS) * g).astype(BF16)

    q = norm(q_ref[...].astype(F32), qg_ref[...])
    k = norm(mk_ref[...], kg_ref[...])
    s = _dot_nt(q, k) * (MEM_HEAD_DIM ** -0.5)
    m = jnp.max(s, axis=-1, keepdims=True)
    p = jnp.exp(s - m)
    p = (p / jnp.sum(p, axis=-1, keepdims=True)).astype(BF16)
    o_ref[...] = _dot(p, mv_ref[...].astype(BF16)).astype(o_ref.dtype)


def memory_attention(proj, q_col_block, mkv, q_gain, k_gain, batch, seq, block_rows=512):
    hd = MEM_HEAD_DIM
    nblk = seq // block_rows
    gain_spec = pl.BlockSpec((1, hd), lambda b, i, h: (0, 0))
    return pl.pallas_call(
        _mem_attn_kernel,
        grid=(batch, nblk, MEM_HEADS),
        in_specs=[pl.BlockSpec((block_rows, hd), lambda b, i, h: (b * nblk + i, q_col_block + h)),
                  pl.BlockSpec((N_MEM, hd), lambda b, i, h: (b, h)),
                  pl.BlockSpec((N_MEM, hd), lambda b, i, h: (b, MEM_HEADS + h)),
                  gain_spec, gain_spec],
        out_specs=pl.BlockSpec((block_rows, hd), lambda b, i, h: (b * nblk + i, h)),
        out_shape=jax.ShapeDtypeStruct((batch * seq, MEM_WIDTH), BF16),
        compiler_params=_params(("parallel", "parallel", "parallel")),
        name="mem_attn",
    )(proj, mkv, mkv, q_gain.reshape(1, hd), k_gain.reshape(1, hd))


def _out_proj_kernel(tok_ref, mem_ref, w_ref, x_ref, o_ref):
    acc = _dot(tok_ref[...], w_ref[:TOKEN_WIDTH, :])
    acc = acc + _dot(mem_ref[...], w_ref[TOKEN_WIDTH:, :])
    o_ref[...] = x_ref[...] + acc


def out_projection(tok, mem, w_o, x):
    t = x.shape[0]
    bm, bn = 1024, 512
    return pl.pallas_call(
        _out_proj_kernel,
        grid=(t // bm, D_MODEL // bn),
        in_specs=[pl.BlockSpec((bm, TOKEN_WIDTH), lambda i, j: (i, 0)),
                  pl.BlockSpec((bm, MEM_WIDTH), lambda i, j: (i, 0)),
                  pl.BlockSpec((D_MODEL, bn), lambda i, j: (0, j)),
                  pl.BlockSpec((bm, bn), lambda i, j: (i, j))],
        out_specs=pl.BlockSpec((bm, bn), lambda i, j: (i, j)),
        out_shape=jax.ShapeDtypeStruct((t, D_MODEL), F32),
        compiler_params=_params(("parallel", "arbitrary")),
        name="out_proj",
    )(tok, mem, w_o, x)


FFN_BM = 1024
FFN_FC = 512
FFN_NC = 256
D_FF_PAD = -(-D_FF // FFN_FC) * FFN_FC
FFN_VMEM_LIMIT = 62 * 1024 * 1024


def _ffn_kernel(xn_ref, wg_ref, wu_ref, wd_ref, x_ref, o_ref, acc_ref, *, n_f):
    f = pl.program_id(1)
    n_slab = D_MODEL // FFN_NC

    @pl.when(f == 0)
    def _():
        acc_ref[...] = jnp.zeros_like(acc_ref)

    @pl.when(f < n_f)
    def _():
        xn = xn_ref[...]
        gate = _dot(xn, wg_ref[...])
        up = _dot(xn, wu_ref[...])
        hid = (_silu(gate) * up).astype(BF16)
        for c in range(n_slab):
            acc_ref[c] += _dot(hid, wd_ref[:, c * FFN_NC:(c + 1) * FFN_NC])

    @pl.when(f >= n_f)
    def _():
        o_ref[...] = x_ref[...] + acc_ref[f - n_f]


def ffn_weights(w_gate_up, w_down):
    pad = D_FF_PAD - D_FF
    wg = jnp.pad(w_gate_up[:, :D_FF].astype(BF16), ((0, 0), (0, pad)))
    wu = jnp.pad(w_gate_up[:, D_FF:].astype(BF16), ((0, 0), (0, pad)))
    wd = jnp.pad(w_down.astype(BF16), ((0, pad), (0, 0)))
    return wg, wu, wd


def ffn(xn, weights, x):
    wg, wu, wd = weights
    t = x.shape[0]
    n_f = D_FF_PAD // FFN_FC
    n_slab = D_MODEL // FFN_NC
    last = n_f - 1

    def fidx(f):
        return jnp.minimum(f, last)

    def oidx(f):
        return jnp.maximum(f - n_f, 0)

    return pl.pallas_call(
        functools.partial(_ffn_kernel, n_f=n_f),
        grid=(t // FFN_BM, n_f + n_slab),
        in_specs=[pl.BlockSpec((FFN_BM, D_MODEL), lambda i, f: (i, 0), pipeline_mode=pl.Buffered(1)),
                  pl.BlockSpec((D_MODEL, FFN_FC), lambda i, f: (0, fidx(f))),
                  pl.BlockSpec((D_MODEL, FFN_FC), lambda i, f: (0, fidx(f))),
                  pl.BlockSpec((FFN_FC, D_MODEL), lambda i, f: (fidx(f), 0)),
                  pl.BlockSpec((FFN_BM, FFN_NC), lambda i, f: (i, oidx(f)))],
        out_specs=pl.BlockSpec((FFN_BM, FFN_NC), lambda i, f: (i, oidx(f))),
        out_shape=jax.ShapeDtypeStruct((t, D_MODEL), F32),
        scratch_shapes=[pltpu.VMEM((n_slab, FFN_BM, FFN_NC), F32)],
        compiler_params=_params(("parallel", "arbitrary"), FFN_VMEM_LIMIT),
        name="ffn",
    )(xn, wg, wu, wd, x)


def _rope_consts(head_dim):
    half = head_dim // 2
    inv_freq = ROPE_THETA ** (-jnp.arange(0, head_dim, 2, dtype=F32) / head_dim)
    reps = LANES // half if half < LANES else 1
    freq = jnp.tile(inv_freq, reps)[:LANES]
    if half >= LANES:
        sign = jnp.ones((LANES,), F32)
    else:
        sign = jnp.tile(jnp.concatenate([-jnp.ones((half,), F32), jnp.ones((half,), F32)]), reps // 2)
    return freq.reshape(1, LANES), sign.reshape(1, LANES)


def kernel(x, mem, positions, mem_norm_g, w_mem_kv, mem_k_norm_g, l0_attn_norm_g, l0_w_in, l0_ret_norm_g, l0_mem_q_norm_g, l0_w_o, l0_ffn_norm_g, l0_w_gate_up, l0_w_down, l1_attn_norm_g, l1_w_in, l1_q_norm_g, l1_k_norm_g, l1_sinks, l1_mem_q_norm_g, l1_w_o, l1_ffn_norm_g, l1_w_gate_up, l1_w_down):
    batch, seq, d = x.shape
    t = batch * seq
    x2 = x.reshape(t, d)

    pos_f = positions.astype(F32).reshape(batch, seq, 1)
    cos_r, sin_r = rope_tables(pos_f, *_rope_consts(RET_HEAD_DIM))
    cos_s, sin_s = rope_tables(pos_f, *_rope_consts(SWA_HEAD_DIM))
    log_g = jnp.log1p(-jnp.exp2(-5.0 - jnp.arange(RET_HEADS, dtype=F32)))

    mem_n = rmsnorm_rows(mem.reshape(batch * N_MEM, d), mem_norm_g)
    mkv = matmul(mem_n, w_mem_kv, F32)

    h = rmsnorm_rows(x2, l0_attn_norm_g)
    proj = matmul(h, l0_w_in, BF16)
    tok = retention_mixer(proj, cos_r, sin_r, log_g, l0_ret_norm_g, batch, seq)
    mo = memory_attention(proj, 4 * TOKEN_WIDTH // MEM_HEAD_DIM, mkv, l0_mem_q_norm_g, mem_k_norm_g,
                          batch, seq)
    x2 = out_projection(tok, mo, l0_w_o.astype(BF16), x2)
    x2 = ffn(rmsnorm_rows(x2, l0_ffn_norm_g), ffn_weights(l0_w_gate_up, l0_w_down), x2)

    h = rmsnorm_rows(x2, l1_attn_norm_g)
    proj = matmul(h, l1_w_in, BF16)
    tok = swa_mixer(proj, cos_s, sin_s, l1_q_norm_g, l1_k_norm_g, l1_sinks, batch, seq)
    q_col = (TOKEN_WIDTH + 2 * SWA_KV_HEADS * SWA_HEAD_DIM) // MEM_HEAD_DIM
    mo = memory_attention(proj, q_col, mkv, l1_mem_q_norm_g, mem_k_norm_g, batch, seq)
    x2 = out_projection(tok, mo, l1_w_o.astype(BF16), x2)
    x2 = ffn(rmsnorm_rows(x2, l1_ffn_norm_g), ffn_weights(l1_w_gate_up, l1_w_down), x2)
    return x2.reshape(batch, seq, d)
```

```python
import functools
import math

import jax
import jax.numpy as jnp
from jax import lax
from jax.experimental import pallas as pl
from jax.experimental.pallas import tpu as pltpu

D_MODEL = 4096
N_MEM = 256
TOKEN_WIDTH = 3 * D_MODEL // 4
MEM_WIDTH = D_MODEL // 4
MEM_HEADS = 4
MEM_HEAD_DIM = MEM_WIDTH // MEM_HEADS
RET_HEAD_DIM = 256
RET_HEADS = TOKEN_WIDTH // RET_HEAD_DIM
RET_CHUNK = 128
SWA_HEAD_DIM = 64
SWA_HEADS = TOKEN_WIDTH // SWA_HEAD_DIM
SWA_GROUP = 8
SWA_KV_HEADS = SWA_HEADS // SWA_GROUP
WINDOW = 128
D_FF = ((8 * D_MODEL + 3 * 256 - 1) // (3 * 256)) * 256
ROPE_THETA = 10000.0
EPS = 1e-6

LANES = 128
VMEM_LIMIT = 56 * 1024 * 1024

BF16 = jnp.bfloat16
F32 = jnp.float32


def _params(semantics, vmem=VMEM_LIMIT):
    return pltpu.CompilerParams(dimension_semantics=semantics, vmem_limit_bytes=vmem)


def _dot(a, b):
    return jnp.dot(a, b, preferred_element_type=F32)


def _dot_nt(a, b):
    return lax.dot_general(a, b, (((1,), (1,)), ((), ())), preferred_element_type=F32)


def _dot_tn(a, b):
    return lax.dot_general(a, b, (((0,), (0,)), ((), ())), preferred_element_type=F32)


def _silu(x):
    return x / (1.0 + jnp.exp(-x))


def _rmsnorm_kernel(x_ref, g_ref, o_ref):
    x = x_ref[...]
    ms = jnp.mean(x * x, axis=-1, keepdims=True)
    o_ref[...] = (x * lax.rsqrt(ms + EPS) * g_ref[...]).astype(o_ref.dtype)


def rmsnorm_rows(x, g, block_rows=512):
    t, d = x.shape
    return pl.pallas_call(
        _rmsnorm_kernel,
        grid=(t // block_rows,),
        in_specs=[pl.BlockSpec((block_rows, d), lambda i: (i, 0)),
                  pl.BlockSpec((1, d), lambda i: (0, 0))],
        out_specs=pl.BlockSpec((block_rows, d), lambda i: (i, 0)),
        out_shape=jax.ShapeDtypeStruct((t, d), BF16),
        compiler_params=_params(("parallel",)),
        name="rmsnorm",
    )(x, g.reshape(1, d))


def _matmul_kernel(x_ref, w_ref, o_ref):
    o_ref[...] = _dot(x_ref[...], w_ref[...].astype(BF16)).astype(o_ref.dtype)


def _scaled_matmul_kernel(x_ref, w_ref, r_ref, o_ref):
    o_ref[...] = (_dot(x_ref[...], w_ref[...].astype(BF16)) * r_ref[...]).astype(o_ref.dtype)


def _pick_block(n, candidates):
    for c in candidates:
        if n % c == 0:
            return c
    raise ValueError(f"no block size for {n}")


def matmul(x, w, out_dtype, row_scale=None):
    m, k = x.shape
    n = w.shape[1]
    bm = _pick_block(m, (2048, 1024))
    bn = _pick_block(n, (512, 256))
    in_specs = [pl.BlockSpec((bm, k), lambda i, j: (i, 0), pipeline_mode=pl.Buffered(1)),
                pl.BlockSpec((k, bn), lambda i, j: (0, j))]
    args = [x, w]
    if row_scale is not None:
        in_specs.append(pl.BlockSpec((bm, 1), lambda i, j: (i, 0)))
        args.append(row_scale)
    return pl.pallas_call(
        _matmul_kernel if row_scale is None else _scaled_matmul_kernel,
        grid=(m // bm, n // bn),
        in_specs=in_specs,
        out_specs=pl.BlockSpec((bm, bn), lambda i, j: (i, j)),
        out_shape=jax.ShapeDtypeStruct((m, n), out_dtype),
        compiler_params=_params(("parallel", "arbitrary")),
        name="matmul",
    )(*args)


def _rope_table_kernel(pos_ref, freq_ref, sign_ref, cos_ref, sin_ref):
    ang = pos_ref[...] * freq_ref[...]
    cos_ref[...] = jnp.cos(ang)
    sin_ref[...] = jnp.sin(ang) * sign_ref[...]


def rope_tables(pos_f, freq, sign):
    b, s, _ = pos_f.shape
    out = jax.ShapeDtypeStruct((b, s, LANES), F32)
    return pl.pallas_call(
        _rope_table_kernel,
        grid=(b,),
        in_specs=[pl.BlockSpec((None, s, 1), lambda i: (i, 0, 0)),
                  pl.BlockSpec((1, LANES), lambda i: (0, 0)),
                  pl.BlockSpec((1, LANES), lambda i: (0, 0))],
        out_specs=[pl.BlockSpec((None, s, LANES), lambda i: (i, 0, 0))] * 2,
        out_shape=[out, out],
        compiler_params=_params(("parallel",)),
        name="rope_tables",
    )(pos_f, freq, sign)


def _retention_kernel(lg_ref, q_ref, k_ref, v_ref, g_ref, cos_ref, sin_ref, gain_ref,
                      o_ref, state_ref, *, seq):
    h = pl.program_id(1)
    lg = lg_ref[h]
    c = RET_CHUNK
    half = RET_HEAD_DIM // 2
    row = lax.broadcasted_iota(jnp.int32, (c, c), 0)
    col = lax.broadcasted_iota(jnp.int32, (c, c), 1)
    diff = (row - col).astype(F32)
    intra = jnp.where(diff >= 0, jnp.exp(jnp.maximum(diff, 0.0) * lg), 0.0)
    idx = lax.broadcasted_iota(jnp.int32, (c, 1), 0).astype(F32)
    q_dec = jnp.exp((idx + 1.0) * lg)
    k_dec = jnp.exp((c - 1.0 - idx) * lg)
    chunk_dec = jnp.exp(jnp.full((1, 1), float(c), F32) * lg)
    gain = gain_ref[...]
    state_ref[...] = jnp.zeros_like(state_ref)

    def rope(x, cos, sin):
        x1, x2 = x[:, :half], x[:, half:]
        return jnp.concatenate([x1 * cos - x2 * sin, x2 * cos + x1 * sin], axis=1)

    def body(n, carry):
        rows = pl.ds(pl.multiple_of(n * c, c), c)
        cos = cos_ref[rows, :]
        sin = sin_ref[rows, :]
        qc = rope(q_ref[rows, :].astype(F32), cos, sin)
        kc = rope(k_ref[rows, :].astype(F32), cos, sin) * (RET_HEAD_DIM ** -0.5)
        vc = v_ref[rows, :]
        scores = _dot_nt(qc.astype(BF16), kc.astype(BF16)) * intra
        o = _dot(scores.astype(BF16), vc)
        state = state_ref[...]
        o = o + _dot((qc * q_dec).astype(BF16), state.astype(BF16))
        state_ref[...] = state * chunk_dec + _dot_tn((kc * k_dec).astype(BF16), vc)
        ms = jnp.mean(o * o, axis=-1, keepdims=True)
        y = o * lax.rsqrt(ms + EPS) * gain
        o_ref[rows, :] = (_silu(g_ref[rows, :].astype(F32)) * y).astype(o_ref.dtype)
        return carry

    lax.fori_loop(0, seq // c, body, 0, unroll=True)


def retention_mixer(proj, cos, sin, log_g, gain, batch, seq):
    hd = RET_HEAD_DIM
    nh = RET_HEADS

    def col_spec(offset):
        return pl.BlockSpec((seq, hd), lambda b, h, lg: (b, offset + h))

    table_spec = pl.BlockSpec((None, seq, LANES), lambda b, h, lg: (b, 0, 0))
    grid_spec = pltpu.PrefetchScalarGridSpec(
        num_scalar_prefetch=1,
        grid=(batch, nh),
        in_specs=[col_spec(0), col_spec(nh), col_spec(2 * nh), col_spec(3 * nh),
                  table_spec, table_spec,
                  pl.BlockSpec((None, 1, hd), lambda b, h, lg: (h, 0, 0))],
        out_specs=pl.BlockSpec((seq, hd), lambda b, h, lg: (b, h)),
        scratch_shapes=[pltpu.VMEM((hd, hd), F32)],
    )
    return pl.pallas_call(
        functools.partial(_retention_kernel, seq=seq),
        grid_spec=grid_spec,
        out_shape=jax.ShapeDtypeStruct((batch * seq, TOKEN_WIDTH), BF16),
        compiler_params=_params(("parallel", "parallel")),
        name="retention",
    )(log_g, proj, proj, proj, proj, cos, sin, gain.reshape(nh, 1, hd))


def _swa_kernel(sink_ref, q_ref, kc_ref, kp_ref, vc_ref, vp_ref, cosc_ref, sinc_ref,
                cosp_ref, sinp_ref, qg_ref, kg_ref, o_ref):
    n = pl.program_id(1)
    w = WINDOW
    dh = SWA_HEAD_DIM
    lane = lax.broadcasted_iota(jnp.int32, (1, LANES), 1)
    low_half = lane < dh
    first_rot = (lane % dh) < (dh // 2)
    r = lax.broadcasted_iota(jnp.int32, (LANES, LANES), 0) // dh
    cidx = lax.broadcasted_iota(jnp.int32, (LANES, LANES), 1) // dh
    ones_bd = (r == cidx).astype(BF16)

    def norm_rope(x, gain, cos, sin):
        ss = x * x
        hi = ss.astype(BF16)
        lo = (ss - hi.astype(F32)).astype(BF16)
        gs = _dot(hi, ones_bd) + _dot(lo, ones_bd)
        y = x * lax.rsqrt(gs * (1.0 / dh) + EPS) * gain
        rot = jnp.where(first_rot, pltpu.roll(y, LANES - dh // 2, 1), pltpu.roll(y, dh // 2, 1))
        return y * cos + rot * sin

    cos_c, sin_c = cosc_ref[...], sinc_ref[...]
    cos_p, sin_p = cosp_ref[...], sinp_ref[...]
    qg, kg = qg_ref[...], kg_ref[...]

    i_idx = lax.broadcasted_iota(jnp.int32, (w, 2 * w), 0)
    j_idx = lax.broadcasted_iota(jnp.int32, (w, 2 * w), 1)
    valid = (j_idx > i_idx) & (j_idx <= i_idx + w) & ((n > 0) | (j_idx >= w))
    valid = jnp.concatenate([valid, valid], axis=0)
    neg = jnp.finfo(F32).min
    scale = dh ** -0.5
    e_row = lax.broadcasted_iota(jnp.int32, (4 * w, LANES), 0) < 2 * w
    e_lane = lax.broadcasted_iota(jnp.int32, (4 * w, LANES), 1) < dh
    e_mat = (e_row == e_lane).astype(BF16)

    n_kv_slabs = SWA_KV_HEADS * dh // LANES
    k_slabs, v_slabs = [], []
    for s in range(n_kv_slabs):
        cols = slice(s * LANES, (s + 1) * LANES)
        k_cur = norm_rope(kc_ref[:, cols].astype(F32), kg, cos_c, sin_c)
        k_prev = norm_rope(kp_ref[:, cols].astype(F32), kg, cos_p, sin_p)
        k_slabs.append(jnp.concatenate([k_prev, k_cur], axis=0))
        v_slabs.append(jnp.concatenate([vp_ref[:, cols], vc_ref[:, cols]], axis=0))

    for kv in range(SWA_KV_HEADS):
        k_slab = k_slabs[kv // 2]
        v_slab = v_slabs[kv // 2]
        if kv % 2 == 0:
            k_own = jnp.where(low_half, k_slab, 0.0)
            v_lo = jnp.where(low_half, v_slab, jnp.zeros_like(v_slab))
            v_hi = pltpu.roll(v_lo.astype(F32), dh, 1).astype(BF16)
        else:
            k_own = jnp.where(low_half, 0.0, k_slab)
            v_hi = jnp.where(low_half, jnp.zeros_like(v_slab), v_slab)
            v_lo = pltpu.roll(v_hi.astype(F32), dh, 1).astype(BF16)
        kk = (k_own + pltpu.roll(k_own, dh, 1)).astype(BF16)
        v_ext = jnp.concatenate([jnp.concatenate([v_lo, v_hi], axis=0), e_mat], axis=1)

        for pair in range(SWA_GROUP // 2):
            col0 = (kv * SWA_GROUP // 2 + pair) * LANES
            qn = norm_rope(q_ref[:, col0:col0 + LANES].astype(F32), qg, cos_c, sin_c) * scale
            q2 = jnp.concatenate([jnp.where(low_half, qn, 0.0), jnp.where(low_half, 0.0, qn)],
                                 axis=0).astype(BF16)
            head = kv * SWA_GROUP + 2 * pair
            sinks = jnp.concatenate([jnp.full((w, 1), sink_ref[head], F32),
                                     jnp.full((w, 1), sink_ref[head + 1], F32)], axis=0)
            s = jnp.where(valid, _dot_nt(q2, kk), neg)
            m = jnp.maximum(jnp.max(s, axis=-1, keepdims=True), sinks)
            p = jnp.exp(s - m).astype(BF16)
            sink_p = jnp.exp(sinks - m)
            ov = _dot(jnp.concatenate([p[:w], p[w:]], axis=1), v_ext)
            denom = ov[:, LANES:] + jnp.where(low_half, sink_p[:w], sink_p[w:])
            o_ref[:, col0:col0 + LANES] = (ov[:, :LANES] / denom).astype(o_ref.dtype)


def swa_mixer(proj, cos, sin, q_gain, k_gain, sinks, batch, seq):
    w = WINDOW
    nb = seq // w
    kv_w = SWA_KV_HEADS * SWA_HEAD_DIM
    k_blk = TOKEN_WIDTH // kv_w

    def cur(b, n, s):
        return b * nb + n

    def prev(b, n, s):
        return b * nb + jnp.maximum(n - 1, 0)

    tab_cur = pl.BlockSpec((None, w, LANES), lambda b, n, s: (b, n, 0))
    tab_prev = pl.BlockSpec((None, w, LANES), lambda b, n, s: (b, jnp.maximum(n - 1, 0), 0))
    gain_spec = pl.BlockSpec((1, LANES), lambda b, n, s: (0, 0))
    grid_spec = pltpu.PrefetchScalarGridSpec(
        num_scalar_prefetch=1,
        grid=(batch, nb),
        in_specs=[pl.BlockSpec((w, TOKEN_WIDTH), lambda b, n, s: (cur(b, n, s), 0)),
                  pl.BlockSpec((w, kv_w), lambda b, n, s: (cur(b, n, s), k_blk)),
                  pl.BlockSpec((w, kv_w), lambda b, n, s: (prev(b, n, s), k_blk)),
                  pl.BlockSpec((w, kv_w), lambda b, n, s: (cur(b, n, s), k_blk + 1)),
                  pl.BlockSpec((w, kv_w), lambda b, n, s: (prev(b, n, s), k_blk + 1)),
                  tab_cur, tab_cur, tab_prev, tab_prev, gain_spec, gain_spec],
        out_specs=pl.BlockSpec((w, TOKEN_WIDTH), lambda b, n, s: (cur(b, n, s), 0)),
    )
    tile = LANES // SWA_HEAD_DIM
    return pl.pallas_call(
        _swa_kernel,
        grid_spec=grid_spec,
        out_shape=jax.ShapeDtypeStruct((batch * seq, TOKEN_WIDTH), BF16),
        compiler_params=_params(("parallel", "parallel")),
        name="swa",
    )(sinks, proj, proj, proj, proj, proj, cos, sin, cos, sin,
      jnp.tile(q_gain, tile).reshape(1, LANES), jnp.tile(k_gain, tile).reshape(1, LANES))


def _mem_attn_kernel(q_ref, mk_ref, mv_ref, qg_ref, kg_ref, o_ref):
    def norm(x, g):
        ms = jnp.mean(x * x, axis=-1, keepdims=True)
        return (x * lax.rsqrt(ms + EPS) * g).astype(BF16)

    q = norm(q_ref[...].astype(F32), qg_ref[...])
    k = norm(mk_ref[...], kg_ref[...])
    s = _dot_nt(q, k) * (MEM_HEAD_DIM ** -0.5)
    m = jnp.max(s, axis=-1, keepdims=True)
    p = jnp.exp(s - m)
    p = (p / jnp.sum(p, axis=-1, keepdims=True)).astype(BF16)
    o_ref[...] = _dot(p, mv_ref[...].astype(BF16)).astype(o_ref.dtype)


def memory_attention(proj, q_col_block, mkv, q_gain, k_gain, batch, seq, block_rows=512):
    hd = MEM_HEAD_DIM
    nblk = seq // block_rows
    gain_spec = pl.BlockSpec((1, hd), lambda b, i, h: (0, 0))
    return pl.pallas_call(
        _mem_attn_kernel,
        grid=(batch, nblk, MEM_HEADS),
        in_specs=[pl.BlockSpec((block_rows, hd), lambda b, i, h: (b * nblk + i, q_col_block + h)),
                  pl.BlockSpec((N_MEM, hd), lambda b, i, h: (b, h)),
                  pl.BlockSpec((N_MEM, hd), lambda b, i, h: (b, MEM_HEADS + h)),
                  gain_spec, gain_spec],
        out_specs=pl.BlockSpec((block_rows, hd), lambda b, i, h: (b * nblk + i, h)),
        out_shape=jax.ShapeDtypeStruct((batch * seq, MEM_WIDTH), BF16),
        compiler_params=_params(("parallel", "parallel", "parallel")),
        name="mem_attn",
    )(proj, mkv, mkv, q_gain.reshape(1, hd), k_gain.reshape(1, hd))


def _norm_operands(x_new, gain, o_ref, xg_ref, rstd_ref, ssq_ref, first, last):
    o_ref[...] = x_new
    xg_ref[...] = (x_new * gain).astype(xg_ref.dtype)
    part = jnp.sum(x_new * x_new, axis=-1, keepdims=True)

    @pl.when(first)
    def _():
        ssq_ref[...] = part

    @pl.when(jnp.logical_not(first))
    def _():
        ssq_ref[...] += part

    @pl.when(last)
    def _():
        rstd_ref[...] = lax.rsqrt(ssq_ref[...] * (1.0 / D_MODEL) + EPS)


def _out_proj_kernel(tok_ref, mem_ref, w_ref, x_ref, g_ref, o_ref, xg_ref, rstd_ref, ssq_ref):
    j = pl.program_id(1)
    acc = _dot(tok_ref[...], w_ref[:TOKEN_WIDTH, :])
    acc = acc + _dot(mem_ref[...], w_ref[TOKEN_WIDTH:, :])
    _norm_operands(x_ref[...] + acc, g_ref[...], o_ref, xg_ref, rstd_ref, ssq_ref,
                   j == 0, j == pl.num_programs(1) - 1)


def out_projection(tok, mem, w_o, x, next_gain):
    t = x.shape[0]
    bm, bn = 1024, 512
    return pl.pallas_call(
        _out_proj_kernel,
        grid=(t // bm, D_MODEL // bn),
        in_specs=[pl.BlockSpec((bm, TOKEN_WIDTH), lambda i, j: (i, 0)),
                  pl.BlockSpec((bm, MEM_WIDTH), lambda i, j: (i, 0)),
                  pl.BlockSpec((D_MODEL, bn), lambda i, j: (0, j)),
                  pl.BlockSpec((bm, bn), lambda i, j: (i, j)),
                  pl.BlockSpec((1, bn), lambda i, j: (0, j))],
        out_specs=[pl.BlockSpec((bm, bn), lambda i, j: (i, j)),
                   pl.BlockSpec((bm, bn), lambda i, j: (i, j)),
                   pl.BlockSpec((bm, 1), lambda i, j: (i, 0))],
        out_shape=[jax.ShapeDtypeStruct((t, D_MODEL), F32),
                   jax.ShapeDtypeStruct((t, D_MODEL), BF16),
                   jax.ShapeDtypeStruct((t, 1), F32)],
        scratch_shapes=[pltpu.VMEM((bm, 1), F32)],
        compiler_params=_params(("parallel", "arbitrary")),
        name="out_proj",
    )(tok, mem, w_o, x, next_gain.reshape(1, D_MODEL))


FFN_BM = 1024
FFN_RM = 512
FFN_FC = 512
FFN_NC = 256
D_FF_PAD = -(-D_FF // FFN_FC) * FFN_FC
FFN_VMEM_LIMIT = 62 * 1024 * 1024
CAST_BLOCK = 256


def _ffn_kernel(xg_ref, rstd_ref, wg_ref, wu_ref, wd_ref, x_ref, *rest, n_f, emit_norm):
    if emit_norm:
        g_ref, o_ref, xg_out_ref, rstd_out_ref, acc_ref, hid_ref, ssq_ref = rest
    else:
        o_ref, acc_ref, hid_ref = rest
    f = pl.program_id(1)
    n_slab = D_MODEL // FFN_NC

    @pl.when(f == 0)
    def _():
        acc_ref[...] = jnp.zeros_like(acc_ref)

    @pl.when(f < n_f)
    def _():
        for r in range(FFN_BM // FFN_RM):
            rows = slice(r * FFN_RM, (r + 1) * FFN_RM)
            xg = xg_ref[rows, :]
            rstd = rstd_ref[rows, :]
            gate = _dot(xg, wg_ref[...]) * rstd
            up = _dot(xg, wu_ref[...]) * rstd
            hid_ref[rows, :] = (_silu(gate) * up).astype(BF16)
        hid = hid_ref[...]
        for c in range(n_slab):
            acc_ref[c] += _dot(hid, wd_ref[:, c * FFN_NC:(c + 1) * FFN_NC])

    @pl.when(f >= n_f)
    def _():
        x_new = x_ref[...] + acc_ref[f - n_f]
        if emit_norm:
            _norm_operands(x_new, g_ref[...], o_ref, xg_out_ref, rstd_out_ref, ssq_ref,
                           f == n_f, f == pl.num_programs(1) - 1)
        else:
            o_ref[...] = x_new


def _cast_pad_kernel(n_valid, *refs):
    ins, outs = refs[:len(refs) // 2], refs[len(refs) // 2:]
    j = pl.program_id(0)
    for i_ref, o_ref in zip(ins, outs):
        @pl.when(j < n_valid)
        def _():
            o_ref[...] = i_ref[...].astype(o_ref.dtype)

        @pl.when(j >= n_valid)
        def _():
            o_ref[...] = jnp.zeros_like(o_ref)


def ffn_weights(w_gate_up, w_down):
    n_valid = D_FF // CAST_BLOCK
    n_blocks = D_FF_PAD // CAST_BLOCK
    last = n_valid - 1
    col_out = pl.BlockSpec((D_MODEL, CAST_BLOCK), lambda j: (0, j))
    wg, wu = pl.pallas_call(
        functools.partial(_cast_pad_kernel, n_valid),
        grid=(n_blocks,),
        in_specs=[pl.BlockSpec((D_MODEL, CAST_BLOCK), lambda j: (0, jnp.minimum(j, last))),
                  pl.BlockSpec((D_MODEL, CAST_BLOCK), lambda j: (0, n_valid + jnp.minimum(j, last)))],
        out_specs=[col_out, col_out],
        out_shape=[jax.ShapeDtypeStruct((D_MODEL, D_FF_PAD), BF16)] * 2,
        compiler_params=_params(("parallel",)),
        name="cast_gate_up",
    )(w_gate_up, w_gate_up)
    wd, = pl.pallas_call(
        functools.partial(_cast_pad_kernel, n_valid),
        grid=(n_blocks,),
        in_specs=[pl.BlockSpec((CAST_BLOCK, D_MODEL), lambda j: (jnp.minimum(j, last), 0))],
        out_specs=[pl.BlockSpec((CAST_BLOCK, D_MODEL), lambda j: (j, 0))],
        out_shape=[jax.ShapeDtypeStruct((D_FF_PAD, D_MODEL), BF16)],
        compiler_params=_params(("parallel",)),
        name="cast_down",
    )(w_down)
    return wg, wu, wd


def ffn(xg, rstd, weights, x, next_gain=None):
    wg, wu, wd = weights
    t = x.shape[0]
    n_f = D_FF_PAD // FFN_FC
    n_slab = D_MODEL // FFN_NC
    last = n_f - 1
    emit_norm = next_gain is not None

    def fidx(f):
        return jnp.minimum(f, last)

    def oidx(f):
        return jnp.maximum(f - n_f, 0)

    slab_spec = pl.BlockSpec((FFN_BM, FFN_NC), lambda i, f: (i, oidx(f)))
    in_specs = [pl.BlockSpec((FFN_BM, D_MODEL), lambda i, f: (i, 0), pipeline_mode=pl.Buffered(1)),
                pl.BlockSpec((FFN_BM, 1), lambda i, f: (i, 0)),
                pl.BlockSpec((D_MODEL, FFN_FC), lambda i, f: (0, fidx(f))),
                pl.BlockSpec((D_MODEL, FFN_FC), lambda i, f: (0, fidx(f))),
                pl.BlockSpec((FFN_FC, D_MODEL), lambda i, f: (fidx(f), 0)),
                slab_spec]
    args = [xg, rstd, wg, wu, wd, x]
    out_specs = [slab_spec]
    out_shape = [jax.ShapeDtypeStruct((t, D_MODEL), F32)]
    scratch = [pltpu.VMEM((n_slab, FFN_BM, FFN_NC), F32), pltpu.VMEM((FFN_BM, FFN_FC), BF16)]
    if emit_norm:
        in_specs.append(pl.BlockSpec((1, FFN_NC), lambda i, f: (0, oidx(f))))
        args.append(next_gain.reshape(1, D_MODEL))
        out_specs += [slab_spec, pl.BlockSpec((FFN_BM, 1), lambda i, f: (i, 0))]
        out_shape += [jax.ShapeDtypeStruct((t, D_MODEL), BF16), jax.ShapeDtypeStruct((t, 1), F32)]
        scratch.append(pltpu.VMEM((FFN_BM, 1), F32))
    out = pl.pallas_call(
        functools.partial(_ffn_kernel, n_f=n_f, emit_norm=emit_norm),
        grid=(t // FFN_BM, n_f + n_slab),
        in_specs=in_specs,
        out_specs=out_specs,
        out_shape=out_shape,
        scratch_shapes=scratch,
        compiler_params=_params(("parallel", "arbitrary"), FFN_VMEM_LIMIT),
        name="ffn",
    )(*args)
    return out if emit_norm else out[0]


def _rope_consts(head_dim):
    half = head_dim // 2
    inv_freq = ROPE_THETA ** (-jnp.arange(0, head_dim, 2, dtype=F32) / head_dim)
    reps = LANES // half if half < LANES else 1
    freq = jnp.tile(inv_freq, reps)[:LANES]
    if half >= LANES:
        sign = jnp.ones((LANES,), F32)
    else:
        sign = jnp.tile(jnp.concatenate([-jnp.ones((half,), F32), jnp.ones((half,), F32)]), reps // 2)
    return freq.reshape(1, LANES), sign.reshape(1, LANES)


def kernel(x, mem, positions, mem_norm_g, w_mem_kv, mem_k_norm_g, l0_attn_norm_g, l0_w_in, l0_ret_norm_g, l0_mem_q_norm_g, l0_w_o, l0_ffn_norm_g, l0_w_gate_up, l0_w_down, l1_attn_norm_g, l1_w_in, l1_q_norm_g, l1_k_norm_g, l1_sinks, l1_mem_q_norm_g, l1_w_o, l1_ffn_norm_g, l1_w_gate_up, l1_w_down):
    batch, seq, d = x.shape
    t = batch * seq
    x2 = x.reshape(t, d)

    pos_f = positions.astype(F32).reshape(batch, seq, 1)
    cos_r, sin_r = rope_tables(pos_f, *_rope_consts(RET_HEAD_DIM))
    cos_s, sin_s = rope_tables(pos_f, *_rope_consts(SWA_HEAD_DIM))
    log_g = jnp.log1p(-jnp.exp2(-5.0 - jnp.arange(RET_HEADS, dtype=F32)))

    mem_n = rmsnorm_rows(mem.reshape(batch * N_MEM, d), mem_norm_g)
    mkv = matmul(mem_n, w_mem_kv, F32)

    h = rmsnorm_rows(x2, l0_attn_norm_g)
    proj = matmul(h, l0_w_in, BF16)
    tok = retention_mixer(proj, cos_r, sin_r, log_g, l0_ret_norm_g, batch, seq)
    mo = memory_attention(proj, 4 * TOKEN_WIDTH // MEM_HEAD_DIM, mkv, l0_mem_q_norm_g, mem_k_norm_g,
                          batch, seq)
    x2, xg, rstd = out_projection(tok, mo, l0_w_o.astype(BF16), x2, l0_ffn_norm_g)
    x2, xg, rstd = ffn(xg, rstd, ffn_weights(l0_w_gate_up, l0_w_down), x2, l1_attn_norm_g)

    proj = matmul(xg, l1_w_in, BF16, rstd)
    tok = swa_mixer(proj, cos_s, sin_s, l1_q_norm_g, l1_k_norm_g, l1_sinks, batch, seq)
    q_col = (TOKEN_WIDTH + 2 * SWA_KV_HEADS * SWA_HEAD_DIM) // MEM_HEAD_DIM
    mo = memory_attention(proj, q_col, mkv, l1_mem_q_norm_g, mem_k_norm_g, batch, seq)
    x2, xg, rstd = out_projection(tok, mo, l1_w_o.astype(BF16), x2, l1_ffn_norm_g)
    x2 = ffn(xg, rstd, ffn_weights(l1_w_gate_up, l1_w_down), x2)
    return x2.reshape(batch, seq, d)
```

```python
import functools
import math

import jax
import jax.numpy as jnp
from jax import lax
from jax.experimental import pallas as pl
from jax.experimental.pallas import tpu as pltpu

D_MODEL = 4096
N_MEM = 256
TOKEN_WIDTH = 3 * D_MODEL // 4
MEM_WIDTH = D_MODEL // 4
MEM_HEADS = 4
MEM_HEAD_DIM = MEM_WIDTH // MEM_HEADS
RET_HEAD_DIM = 256
RET_HEADS = TOKEN_WIDTH // RET_HEAD_DIM
RET_CHUNK = 128
SWA_HEAD_DIM = 64
SWA_HEADS = TOKEN_WIDTH // SWA_HEAD_DIM
SWA_GROUP = 8
SWA_KV_HEADS = SWA_HEADS // SWA_GROUP
WINDOW = 128
D_FF = ((8 * D_MODEL + 3 * 256 - 1) // (3 * 256)) * 256
ROPE_THETA = 10000.0
EPS = 1e-6

LANES = 128
VMEM_LIMIT = 56 * 1024 * 1024

BF16 = jnp.bfloat16
F32 = jnp.float32


def _params(semantics, vmem=VMEM_LIMIT):
    return pltpu.CompilerParams(dimension_semantics=semantics, vmem_limit_bytes=vmem)


def _dot(a, b):
    return jnp.dot(a, b, preferred_element_type=F32)


def _dot_nt(a, b):
    return lax.dot_general(a, b, (((1,), (1,)), ((), ())), preferred_element_type=F32)


def _dot_tn(a, b):
    return lax.dot_general(a, b, (((0,), (0,)), ((), ())), preferred_element_type=F32)


def _silu(x):
    return x / (1.0 + jnp.exp(-x))


def _rmsnorm_kernel(x_ref, g_ref, o_ref):
    x = x_ref[...]
    ms = jnp.mean(x * x, axis=-1, keepdims=True)
    o_ref[...] = (x * lax.rsqrt(ms + EPS) * g_ref[...]).astype(o_ref.dtype)


def rmsnorm_rows(x, g, block_rows=512):
    t, d = x.shape
    return pl.pallas_call(
        _rmsnorm_kernel,
        grid=(t // block_rows,),
        in_specs=[pl.BlockSpec((block_rows, d), lambda i: (i, 0)),
                  pl.BlockSpec((1, d), lambda i: (0, 0))],
        out_specs=pl.BlockSpec((block_rows, d), lambda i: (i, 0)),
        out_shape=jax.ShapeDtypeStruct((t, d), BF16),
        compiler_params=_params(("parallel",)),
        name="rmsnorm",
    )(x, g.reshape(1, d))


def _matmul_kernel(x_ref, w_ref, o_ref):
    o_ref[...] = _dot(x_ref[...], w_ref[...].astype(BF16)).astype(o_ref.dtype)


def _scaled_matmul_kernel(x_ref, w_ref, r_ref, o_ref):
    o_ref[...] = (_dot(x_ref[...], w_ref[...].astype(BF16)) * r_ref[...]).astype(o_ref.dtype)


def _pick_block(n, candidates):
    for c in candidates:
        if n % c == 0:
            return c
    raise ValueError(f"no block size for {n}")


def matmul(x, w, out_dtype, row_scale=None):
    m, k = x.shape
    n = w.shape[1]
    bm = _pick_block(m, (2048, 1024))
    bn = _pick_block(n, (512, 256))
    in_specs = [pl.BlockSpec((bm, k), lambda i, j: (i, 0), pipeline_mode=pl.Buffered(1)),
                pl.BlockSpec((k, bn), lambda i, j: (0, j))]
    args = [x, w]
    if row_scale is not None:
        in_specs.append(pl.BlockSpec((bm, 1), lambda i, j: (i, 0)))
        args.append(row_scale)
    return pl.pallas_call(
        _matmul_kernel if row_scale is None else _scaled_matmul_kernel,
        grid=(m // bm, n // bn),
        in_specs=in_specs,
        out_specs=pl.BlockSpec((bm, bn), lambda i, j: (i, j)),
        out_shape=jax.ShapeDtypeStruct((m, n), out_dtype),
        compiler_params=_params(("parallel", "arbitrary")),
        name="matmul",
    )(*args)


def _rope_table_kernel(pos_ref, freq_ref, sign_ref, cos_ref, sin_ref):
    ang = pos_ref[...] * freq_ref[...]
    cos_ref[...] = jnp.cos(ang)
    sin_ref[...] = jnp.sin(ang) * sign_ref[...]


def rope_tables(pos_f, freq, sign):
    b, s, _ = pos_f.shape
    out = jax.ShapeDtypeStruct((b, s, LANES), F32)
    return pl.pallas_call(
        _rope_table_kernel,
        grid=(b,),
        in_specs=[pl.BlockSpec((None, s, 1), lambda i: (i, 0, 0)),
                  pl.BlockSpec((1, LANES), lambda i: (0, 0)),
                  pl.BlockSpec((1, LANES), lambda i: (0, 0))],
        out_specs=[pl.BlockSpec((None, s, LANES), lambda i: (i, 0, 0))] * 2,
        out_shape=[out, out],
        compiler_params=_params(("parallel",)),
        name="rope_tables",
    )(pos_f, freq, sign)


def _retention_kernel(lg_ref, q_ref, k_ref, v_ref, g_ref, cos_ref, sin_ref, gain_ref,
                      o_ref, state_ref, *, seq):
    h = pl.program_id(1)
    lg = lg_ref[h]
    c = RET_CHUNK
    half = RET_HEAD_DIM // 2
    row = lax.broadcasted_iota(jnp.int32, (c, c), 0)
    col = lax.broadcasted_iota(jnp.int32, (c, c), 1)
    diff = (row - col).astype(F32)
    intra = jnp.where(diff >= 0, jnp.exp(jnp.maximum(diff, 0.0) * lg), 0.0)
    idx = lax.broadcasted_iota(jnp.int32, (c, 1), 0).astype(F32)
    q_dec = jnp.exp((idx + 1.0) * lg)
    k_dec = jnp.exp((c - 1.0 - idx) * lg)
    chunk_dec = jnp.exp(jnp.full((1, 1), float(c), F32) * lg)
    gain = gain_ref[...]
    state_ref[...] = jnp.zeros_like(state_ref)

    def rope(x, cos, sin):
        x1, x2 = x[:, :half], x[:, half:]
        return jnp.concatenate([x1 * cos - x2 * sin, x2 * cos + x1 * sin], axis=1)

    def body(n, carry):
        rows = pl.ds(pl.multiple_of(n * c, c), c)
        cos = cos_ref[rows, :]
        sin = sin_ref[rows, :]
        qc = rope(q_ref[rows, :].astype(F32), cos, sin)
        kc = rope(k_ref[rows, :].astype(F32), cos, sin) * (RET_HEAD_DIM ** -0.5)
        vc = v_ref[rows, :]
        scores = _dot_nt(qc.astype(BF16), kc.astype(BF16)) * intra
        o = _dot(scores.astype(BF16), vc)
        state = state_ref[...]
        o = o + _dot((qc * q_dec).astype(BF16), state.astype(BF16))
        state_ref[...] = state * chunk_dec + _dot_tn((kc * k_dec).astype(BF16), vc)
        ms = jnp.mean(o * o, axis=-1, keepdims=True)
        y = o * lax.rsqrt(ms + EPS) * gain
        o_ref[rows, :] = (_silu(g_ref[rows, :].astype(F32)) * y).astype(o_ref.dtype)
        return carry

    lax.fori_loop(0, seq // c, body, 0, unroll=True)


def retention_mixer(proj, cos, sin, log_g, gain, batch, seq):
    hd = RET_HEAD_DIM
    nh = RET_HEADS

    def col_spec(offset):
        return pl.BlockSpec((seq, hd), lambda b, h, lg: (b, offset + h))

    table_spec = pl.BlockSpec((None, seq, LANES), lambda b, h, lg: (b, 0, 0))
    grid_spec = pltpu.PrefetchScalarGridSpec(
        num_scalar_prefetch=1,
        grid=(batch, nh),
        in_specs=[col_spec(0), col_spec(nh), col_spec(2 * nh), col_spec(3 * nh),
                  table_spec, table_spec,
                  pl.BlockSpec((None, 1, hd), lambda b, h, lg: (h, 0, 0))],
        out_specs=pl.BlockSpec((seq, hd), lambda b, h, lg: (b, h)),
        scratch_shapes=[pltpu.VMEM((hd, hd), F32)],
    )
    return pl.pallas_call(
        functools.partial(_retention_kernel, seq=seq),
        grid_spec=grid_spec,
        out_shape=jax.ShapeDtypeStruct((batch * seq, TOKEN_WIDTH), BF16),
        compiler_params=_params(("parallel", "parallel")),
        name="retention",
    )(log_g, proj, proj, proj, proj, cos, sin, gain.reshape(nh, 1, hd))


def _swa_kernel(sink_ref, q_ref, kc_ref, kp_ref, vc_ref, vp_ref, cosc_ref, sinc_ref,
                cosp_ref, sinp_ref, qg_ref, kg_ref, o_ref):
    n = pl.program_id(1)
    w = WINDOW
    dh = SWA_HEAD_DIM
    lane = lax.broadcasted_iota(jnp.int32, (1, LANES), 1)
    low_half = lane < dh
    first_rot = (lane % dh) < (dh // 2)
    r = lax.broadcasted_iota(jnp.int32, (LANES, LANES), 0) // dh
    cidx = lax.broadcasted_iota(jnp.int32, (LANES, LANES), 1) // dh
    ones_bd = (r == cidx).astype(BF16)

    def norm_rope(x, gain, cos, sin):
        ss = x * x
        hi = ss.astype(BF16)
        lo = (ss - hi.astype(F32)).astype(BF16)
        gs = _dot(hi, ones_bd) + _dot(lo, ones_bd)
        y = x * lax.rsqrt(gs * (1.0 / dh) + EPS) * gain
        rot = jnp.where(first_rot, pltpu.roll(y, LANES - dh // 2, 1), pltpu.roll(y, dh // 2, 1))
        return y * cos + rot * sin

    cos_c, sin_c = cosc_ref[...], sinc_ref[...]
    cos_p, sin_p = cosp_ref[...], sinp_ref[...]
    scale = dh ** -0.5
    qg, kg = qg_ref[...] * scale, kg_ref[...]

    i_idx = lax.broadcasted_iota(jnp.int32, (w, 2 * w), 0)
    j_idx = lax.broadcasted_iota(jnp.int32, (w, 2 * w), 1)
    valid = (j_idx > i_idx) & (j_idx <= i_idx + w) & ((n > 0) | (j_idx >= w))
    bias = jnp.where(valid, 0.0, jnp.finfo(F32).min)
    bias = jnp.concatenate([bias, bias], axis=0)
    sink_col = lane == 0
    first_key = lax.broadcasted_iota(jnp.int32, (2 * w, 1), 0) == 0
    e_row = lax.broadcasted_iota(jnp.int32, (4 * w, LANES), 0) < 2 * w
    e_lane = lax.broadcasted_iota(jnp.int32, (4 * w, LANES), 1) < dh
    e_mat = (e_row == e_lane).astype(BF16)

    n_kv_slabs = SWA_KV_HEADS * dh // LANES
    k_slabs, v_slabs = [], []
    for s in range(n_kv_slabs):
        cols = slice(s * LANES, (s + 1) * LANES)
        k_cur = norm_rope(kc_ref[:, cols].astype(F32), kg, cos_c, sin_c)
        k_prev = norm_rope(kp_ref[:, cols].astype(F32), kg, cos_p, sin_p)
        k_slabs.append(jnp.concatenate([k_prev, k_cur], axis=0))
        v_slab = jnp.concatenate([vp_ref[:, cols], vc_ref[:, cols]], axis=0)
        v_slabs.append(jnp.where(first_key, jnp.zeros_like(v_slab), v_slab))

    for kv in range(SWA_KV_HEADS):
        k_slab = k_slabs[kv // 2]
        v_slab = v_slabs[kv // 2]
        if kv % 2 == 0:
            k_own = jnp.where(low_half, k_slab, 0.0)
            v_lo = jnp.where(low_half, v_slab, jnp.zeros_like(v_slab))
            v_hi = pltpu.roll(v_lo.astype(F32), dh, 1).astype(BF16)
        else:
            k_own = jnp.where(low_half, 0.0, k_slab)
            v_hi = jnp.where(low_half, jnp.zeros_like(v_slab), v_slab)
            v_lo = pltpu.roll(v_hi.astype(F32), dh, 1).astype(BF16)
        kk = (k_own + pltpu.roll(k_own, dh, 1)).astype(BF16)
        v_ext = jnp.concatenate([jnp.concatenate([v_lo, v_hi], axis=0), e_mat], axis=1)

        for pair in range(SWA_GROUP // 2):
            col0 = (kv * SWA_GROUP // 2 + pair) * LANES
            qn = norm_rope(q_ref[:, col0:col0 + LANES].astype(F32), qg, cos_c, sin_c)
            q2 = jnp.concatenate([jnp.where(low_half, qn, 0.0), jnp.where(low_half, 0.0, qn)],
                                 axis=0).astype(BF16)
            head = kv * SWA_GROUP + 2 * pair
            s = _dot_nt(q2, kk) + bias
            s_a = jnp.where(sink_col, sink_ref[head], s[:w, :LANES])
            s_b = jnp.where(sink_col, sink_ref[head + 1], s[w:, :LANES])
            s = jnp.concatenate([jnp.concatenate([s_a, s[:w, LANES:]], axis=1),
                                 jnp.concatenate([s_b, s[w:, LANES:]], axis=1)], axis=0)
            p = jnp.exp(s - jnp.max(s, axis=-1, keepdims=True)).astype(BF16)
            ov = _dot(jnp.concatenate([p[:w], p[w:]], axis=1), v_ext)
            o_ref[:, col0:col0 + LANES] = (ov[:, :LANES] / ov[:, LANES:]).astype(o_ref.dtype)


def swa_mixer(proj, cos, sin, q_gain, k_gain, sinks, batch, seq):
    w = WINDOW
    nb = seq // w
    kv_w = SWA_KV_HEADS * SWA_HEAD_DIM
    k_blk = TOKEN_WIDTH // kv_w

    def cur(b, n, s):
        return b * nb + n

    def prev(b, n, s):
        return b * nb + jnp.maximum(n - 1, 0)

    tab_cur = pl.BlockSpec((None, w, LANES), lambda b, n, s: (b, n, 0))
    tab_prev = pl.BlockSpec((None, w, LANES), lambda b, n, s: (b, jnp.maximum(n - 1, 0), 0))
    gain_spec = pl.BlockSpec((1, LANES), lambda b, n, s: (0, 0))
    grid_spec = pltpu.PrefetchScalarGridSpec(
        num_scalar_prefetch=1,
        grid=(batch, nb),
        in_specs=[pl.BlockSpec((w, TOKEN_WIDTH), lambda b, n, s: (cur(b, n, s), 0)),
                  pl.BlockSpec((w, kv_w), lambda b, n, s: (cur(b, n, s), k_blk)),
                  pl.BlockSpec((w, kv_w), lambda b, n, s: (prev(b, n, s), k_blk)),
                  pl.BlockSpec((w, kv_w), lambda b, n, s: (cur(b, n, s), k_blk + 1)),
                  pl.BlockSpec((w, kv_w), lambda b, n, s: (prev(b, n, s), k_blk + 1)),
                  tab_cur, tab_cur, tab_prev, tab_prev, gain_spec, gain_spec],
        out_specs=pl.BlockSpec((w, TOKEN_WIDTH), lambda b, n, s: (cur(b, n, s), 0)),
    )
    tile = LANES // SWA_HEAD_DIM
    return pl.pallas_call(
        _swa_kernel,
        grid_spec=grid_spec,
        out_shape=jax.ShapeDtypeStruct((batch * seq, TOKEN_WIDTH), BF16),
        compiler_params=_params(("parallel", "parallel")),
        name="swa",
    )(sinks, proj, proj, proj, proj, proj, cos, sin, cos, sin,
      jnp.tile(q_gain, tile).reshape(1, LANES), jnp.tile(k_gain, tile).reshape(1, LANES))


def _mem_attn_kernel(q_ref, mk_ref, mv_ref, qg_ref, kg_ref, o_ref):
    def norm(x, g):
        ms = jnp.mean(x * x, axis=-1, keepdims=True)
        return (x * lax.rsqrt(ms + EPS) * g).astype(BF16)

    q = norm(q_ref[...].astype(F32), qg_ref[...])
    k = norm(mk_ref[...], kg_ref[...])
    s = _dot_nt(q, k) * (MEM_HEAD_DIM ** -0.5)
    m = jnp.max(s, axis=-1, keepdims=True)
    p = jnp.exp(s - m)
    p = (p / jnp.sum(p, axis=-1, keepdims=True)).astype(BF16)
    o_ref[...] = _dot(p, mv_ref[...].astype(BF16)).astype(o_ref.dtype)


def memory_attention(proj, q_col_block, mkv, q_gain, k_gain, batch, seq, block_rows=512):
    hd = MEM_HEAD_DIM
    nblk = seq // block_rows
    gain_spec = pl.BlockSpec((1, hd), lambda b, i, h: (0, 0))
    return pl.pallas_call(
        _mem_attn_kernel,
        grid=(batch, nblk, MEM_HEADS),
        in_specs=[pl.BlockSpec((block_rows, hd), lambda b, i, h: (b * nblk + i, q_col_block + h)),
                  pl.BlockSpec((N_MEM, hd), lambda b, i, h: (b, h)),
                  pl.BlockSpec((N_MEM, hd), lambda b, i, h: (b, MEM_HEADS + h)),
                  gain_spec, gain_spec],
        out_specs=pl.BlockSpec((block_rows, hd), lambda b, i, h: (b * nblk + i, h)),
        out_shape=jax.ShapeDtypeStruct((batch * seq, MEM_WIDTH), BF16),
        compiler_params=_params(("parallel", "parallel", "parallel")),
        name="mem_attn",
    )(proj, mkv, mkv, q_gain.reshape(1, hd), k_gain.reshape(1, hd))


def _norm_operands(x_new, gain, o_ref, xg_ref, rstd_ref, ssq_ref, first, last):
    o_ref[...] = x_new
    xg_ref[...] = (x_new * gain).astype(xg_ref.dtype)
    part = jnp.sum(x_new * x_new, axis=-1, keepdims=True)

    @pl.when(first)
    def _():
        ssq_ref[...] = part

    @pl.when(jnp.logical_not(first))
    def _():
        ssq_ref[...] += part

    @pl.when(last)
    def _():
        rstd_ref[...] = lax.rsqrt(ssq_ref[...] * (1.0 / D_MODEL) + EPS)


def _out_proj_kernel(tok_ref, mem_ref, w_ref, x_ref, g_ref, o_ref, xg_ref, rstd_ref, ssq_ref):
    j = pl.program_id(1)
    acc = _dot(tok_ref[...], w_ref[:TOKEN_WIDTH, :])
    acc = acc + _dot(mem_ref[...], w_ref[TOKEN_WIDTH:, :])
    _norm_operands(x_ref[...] + acc, g_ref[...], o_ref, xg_ref, rstd_ref, ssq_ref,
                   j == 0, j == pl.num_programs(1) - 1)


def out_projection(tok, mem, w_o, x, next_gain):
    t = x.shape[0]
    bm, bn = 1024, 512
    return pl.pallas_call(
        _out_proj_kernel,
        grid=(t // bm, D_MODEL // bn),
        in_specs=[pl.BlockSpec((bm, TOKEN_WIDTH), lambda i, j: (i, 0)),
                  pl.BlockSpec((bm, MEM_WIDTH), lambda i, j: (i, 0)),
                  pl.BlockSpec((D_MODEL, bn), lambda i, j: (0, j)),
                  pl.BlockSpec((bm, bn), lambda i, j: (i, j)),
                  pl.BlockSpec((1, bn), lambda i, j: (0, j))],
        out_specs=[pl.BlockSpec((bm, bn), lambda i, j: (i, j)),
                   pl.BlockSpec((bm, bn), lambda i, j: (i, j)),
                   pl.BlockSpec((bm, 1), lambda i, j: (i, 0))],
        out_shape=[jax.ShapeDtypeStruct((t, D_MODEL), F32),
                   jax.ShapeDtypeStruct((t, D_MODEL), BF16),
                   jax.ShapeDtypeStruct((t, 1), F32)],
        scratch_shapes=[pltpu.VMEM((bm, 1), F32)],
        compiler_params=_params(("parallel", "arbitrary")),
        name="out_proj",
    )(tok, mem, w_o, x, next_gain.reshape(1, D_MODEL))


FFN_BM = 1024
FFN_RM = 512
FFN_FC = 256
FFN_NC = 256
FFN_VMEM_LIMIT = 62 * 1024 * 1024


def _ffn_kernel(xg_ref, rstd_ref, wg_ref, wu_ref, wd_ref, x_ref, *rest, n_f, emit_norm):
    if emit_norm:
        g_ref, o_ref, xg_out_ref, rstd_out_ref, acc_ref, hid_ref, ssq_ref = rest
    else:
        o_ref, acc_ref, hid_ref = rest
    f = pl.program_id(1)
    n_slab = D_MODEL // FFN_NC

    @pl.when(f == 0)
    def _():
        acc_ref[...] = jnp.zeros_like(acc_ref)

    @pl.when(f < n_slab)
    def _():
        acc_ref[f] += x_ref[...]

    @pl.when(f < n_f)
    def _():
        wg = wg_ref[...].astype(BF16)
        wu = wu_ref[...].astype(BF16)
        for r in range(FFN_BM // FFN_RM):
            rows = slice(r * FFN_RM, (r + 1) * FFN_RM)
            xg = xg_ref[rows, :]
            rstd = rstd_ref[rows, :]
            gate = _dot(xg, wg) * rstd
            up = _dot(xg, wu) * rstd
            hid_ref[rows, :] = (_silu(gate) * up).astype(BF16)
        hid = hid_ref[...]
        for c in range(n_slab):
            acc_ref[c] += _dot(hid, wd_ref[:, c * FFN_NC:(c + 1) * FFN_NC].astype(BF16))

    @pl.when(f >= n_f)
    def _():
        x_new = acc_ref[f - n_f]
        if emit_norm:
            _norm_operands(x_new, g_ref[...], o_ref, xg_out_ref, rstd_out_ref, ssq_ref,
                           f == n_f, f == pl.num_programs(1) - 1)
        else:
            o_ref[...] = x_new


def ffn(xg, rstd, w_gate_up, w_down, x, next_gain=None):
    t = x.shape[0]
    n_f = D_FF // FFN_FC
    n_slab = D_MODEL // FFN_NC
    last = n_f - 1
    emit_norm = next_gain is not None

    def fidx(f):
        return jnp.minimum(f, last)

    def oidx(f):
        return jnp.maximum(f - n_f, 0)

    slab_spec = pl.BlockSpec((FFN_BM, FFN_NC), lambda i, f: (i, oidx(f)))
    x_spec = pl.BlockSpec((FFN_BM, FFN_NC), lambda i, f: (i, jnp.minimum(f, n_slab - 1)))
    in_specs = [pl.BlockSpec((FFN_BM, D_MODEL), lambda i, f: (i, 0), pipeline_mode=pl.Buffered(1)),
                pl.BlockSpec((FFN_BM, 1), lambda i, f: (i, 0)),
                pl.BlockSpec((D_MODEL, FFN_FC), lambda i, f: (0, fidx(f))),
                pl.BlockSpec((D_MODEL, FFN_FC), lambda i, f: (0, n_f + fidx(f))),
                pl.BlockSpec((FFN_FC, D_MODEL), lambda i, f: (fidx(f), 0)),
                x_spec]
    args = [xg, rstd, w_gate_up, w_gate_up, w_down, x]
    out_specs = [slab_spec]
    out_shape = [jax.ShapeDtypeStruct((t, D_MODEL), F32)]
    scratch = [pltpu.VMEM((n_slab, FFN_BM, FFN_NC), F32), pltpu.VMEM((FFN_BM, FFN_FC), BF16)]
    if emit_norm:
        in_specs.append(pl.BlockSpec((1, FFN_NC), lambda i, f: (0, oidx(f))))
        args.append(next_gain.reshape(1, D_MODEL))
        out_specs += [slab_spec, pl.BlockSpec((FFN_BM, 1), lambda i, f: (i, 0))]
        out_shape += [jax.ShapeDtypeStruct((t, D_MODEL), BF16), jax.ShapeDtypeStruct((t, 1), F32)]
        scratch.append(pltpu.VMEM((FFN_BM, 1), F32))
    out = pl.pallas_call(
        functools.partial(_ffn_kernel, n_f=n_f, emit_norm=emit_norm),
        grid=(t // FFN_BM, n_f + n_slab),
        in_specs=in_specs,
        out_specs=out_specs,
        out_shape=out_shape,
        scratch_shapes=scratch,
        compiler_params=_params(("parallel", "arbitrary"), FFN_VMEM_LIMIT),
        name="ffn",
    )(*args)
    return out if emit_norm else out[0]


def _rope_consts(head_dim):
    half = head_dim // 2
    inv_freq = ROPE_THETA ** (-jnp.arange(0, head_dim, 2, dtype=F32) / head_dim)
    reps = LANES // half if half < LANES else 1
    freq = jnp.tile(inv_freq, reps)[:LANES]
    if half >= LANES:
        sign = jnp.ones((LANES,), F32)
    else:
        sign = jnp.tile(jnp.concatenate([-jnp.ones((half,), F32), jnp.ones((half,), F32)]), reps // 2)
    return freq.reshape(1, LANES), sign.reshape(1, LANES)


def kernel(x, mem, positions, mem_norm_g, w_mem_kv, mem_k_norm_g, l0_attn_norm_g, l0_w_in, l0_ret_norm_g, l0_mem_q_norm_g, l0_w_o, l0_ffn_norm_g, l0_w_gate_up, l0_w_down, l1_attn_norm_g, l1_w_in, l1_q_norm_g, l1_k_norm_g, l1_sinks, l1_mem_q_norm_g, l1_w_o, l1_ffn_norm_g, l1_w_gate_up, l1_w_down):
    batch, seq, d = x.shape
    t = batch * seq
    x2 = x.reshape(t, d)

    pos_f = positions.astype(F32).reshape(batch, seq, 1)
    cos_r, sin_r = rope_tables(pos_f, *_rope_consts(RET_HEAD_DIM))
    cos_s, sin_s = rope_tables(pos_f, *_rope_consts(SWA_HEAD_DIM))
    log_g = jnp.log1p(-jnp.exp2(-5.0 - jnp.arange(RET_HEADS, dtype=F32)))

    mem_n = rmsnorm_rows(mem.reshape(batch * N_MEM, d), mem_norm_g)
    mkv = matmul(mem_n, w_mem_kv, F32)

    h = rmsnorm_rows(x2, l0_attn_norm_g)
    proj = matmul(h, l0_w_in, BF16)
    tok = retention_mixer(proj, cos_r, sin_r, log_g, l0_ret_norm_g, batch, seq)
    mo = memory_attention(proj, 4 * TOKEN_WIDTH // MEM_HEAD_DIM, mkv, l0_mem_q_norm_g, mem_k_norm_g,
                          batch, seq)
    x2, xg, rstd = out_projection(tok, mo, l0_w_o.astype(BF16), x2, l0_ffn_norm_g)
    x2, xg, rstd = ffn(xg, rstd, l0_w_gate_up, l0_w_down, x2, l1_attn_norm_g)

    proj = matmul(xg, l1_w_in, BF16, rstd)
    tok = swa_mixer(proj, cos_s, sin_s, l1_q_norm_g, l1_k_norm_g, l1_sinks, batch, seq)
    q_col = (TOKEN_WIDTH + 2 * SWA_KV_HEADS * SWA_HEAD_DIM) // MEM_HEAD_DIM
    mo = memory_attention(proj, q_col, mkv, l1_mem_q_norm_g, mem_k_norm_g, batch, seq)
    x2, xg, rstd = out_projection(tok, mo, l1_w_o.astype(BF16), x2, l1_ffn_norm_g)
    x2 = ffn(xg, rstd, l1_w_gate_up, l1_w_down, x2)
    return x2.reshape(batch, seq, d)
```

```python
import functools
import math

import jax
import jax.numpy as jnp
from jax import lax
from jax.experimental import pallas as pl
from jax.experimental.pallas import tpu as pltpu

D_MODEL = 4096
N_MEM = 256
TOKEN_WIDTH = 3 * D_MODEL // 4
MEM_WIDTH = D_MODEL // 4
MEM_HEADS = 4
MEM_HEAD_DIM = MEM_WIDTH // MEM_HEADS
RET_HEAD_DIM = 256
RET_HEADS = TOKEN_WIDTH // RET_HEAD_DIM
RET_CHUNK = 128
SWA_HEAD_DIM = 64
SWA_HEADS = TOKEN_WIDTH // SWA_HEAD_DIM
SWA_GROUP = 8
SWA_KV_HEADS = SWA_HEADS // SWA_GROUP
WINDOW = 128
D_FF = ((8 * D_MODEL + 3 * 256 - 1) // (3 * 256)) * 256
ROPE_THETA = 10000.0
EPS = 1e-6

LANES = 128
VMEM_LIMIT = 56 * 1024 * 1024

BF16 = jnp.bfloat16
F32 = jnp.float32


def _params(semantics, vmem=VMEM_LIMIT):
    return pltpu.CompilerParams(dimension_semantics=semantics, vmem_limit_bytes=vmem)


def _dot(a, b):
    return jnp.dot(a, b, preferred_element_type=F32)


def _dot_nt(a, b):
    return lax.dot_general(a, b, (((1,), (1,)), ((), ())), preferred_element_type=F32)


def _dot_tn(a, b):
    return lax.dot_general(a, b, (((0,), (0,)), ((), ())), preferred_element_type=F32)


def _silu(x):
    return x / (1.0 + jnp.exp(-x))


def _rmsnorm_kernel(x_ref, g_ref, o_ref):
    x = x_ref[...]
    ms = jnp.mean(x * x, axis=-1, keepdims=True)
    o_ref[...] = (x * lax.rsqrt(ms + EPS) * g_ref[...]).astype(o_ref.dtype)


def rmsnorm_rows(x, g, block_rows=512):
    t, d = x.shape
    return pl.pallas_call(
        _rmsnorm_kernel,
        grid=(t // block_rows,),
        in_specs=[pl.BlockSpec((block_rows, d), lambda i: (i, 0)),
                  pl.BlockSpec((1, d), lambda i: (0, 0))],
        out_specs=pl.BlockSpec((block_rows, d), lambda i: (i, 0)),
        out_shape=jax.ShapeDtypeStruct((t, d), BF16),
        compiler_params=_params(("parallel",)),
        name="rmsnorm",
    )(x, g.reshape(1, d))


def _matmul_kernel(x_ref, w_ref, o_ref):
    o_ref[...] = _dot(x_ref[...], w_ref[...].astype(BF16)).astype(o_ref.dtype)


def _scaled_matmul_kernel(x_ref, w_ref, r_ref, o_ref):
    o_ref[...] = (_dot(x_ref[...], w_ref[...].astype(BF16)) * r_ref[...]).astype(o_ref.dtype)


def _pick_block(n, candidates):
    for c in candidates:
        if n % c == 0:
            return c
    raise ValueError(f"no block size for {n}")


def matmul(x, w, out_dtype, row_scale=None):
    m, k = x.shape
    n = w.shape[1]
    bm = _pick_block(m, (2048, 1024))
    bn = _pick_block(n, (512, 256))
    in_specs = [pl.BlockSpec((bm, k), lambda i, j: (i, 0), pipeline_mode=pl.Buffered(1)),
                pl.BlockSpec((k, bn), lambda i, j: (0, j))]
    args = [x, w]
    if row_scale is not None:
        in_specs.append(pl.BlockSpec((bm, 1), lambda i, j: (i, 0)))
        args.append(row_scale)
    return pl.pallas_call(
        _matmul_kernel if row_scale is None else _scaled_matmul_kernel,
        grid=(m // bm, n // bn),
        in_specs=in_specs,
        out_specs=pl.BlockSpec((bm, bn), lambda i, j: (i, j)),
        out_shape=jax.ShapeDtypeStruct((m, n), out_dtype),
        compiler_params=_params(("parallel", "arbitrary")),
        name="matmul",
    )(*args)


def _rope_table_kernel(pos_ref, freq_ref, sign_ref, cos_ref, sin_ref):
    ang = pos_ref[...] * freq_ref[...]
    cos_ref[...] = jnp.cos(ang)
    sin_ref[...] = jnp.sin(ang) * sign_ref[...]


def rope_tables(pos_f, freq, sign):
    b, s, _ = pos_f.shape
    out = jax.ShapeDtypeStruct((b, s, LANES), F32)
    return pl.pallas_call(
        _rope_table_kernel,
        grid=(b,),
        in_specs=[pl.BlockSpec((None, s, 1), lambda i: (i, 0, 0)),
                  pl.BlockSpec((1, LANES), lambda i: (0, 0)),
                  pl.BlockSpec((1, LANES), lambda i: (0, 0))],
        out_specs=[pl.BlockSpec((None, s, LANES), lambda i: (i, 0, 0))] * 2,
        out_shape=[out, out],
        compiler_params=_params(("parallel",)),
        name="rope_tables",
    )(pos_f, freq, sign)


def _retention_kernel(lg_ref, q_ref, k_ref, v_ref, g_ref, cos_ref, sin_ref, gain_ref,
                      o_ref, state_ref, *, seq):
    h = pl.program_id(1)
    lg = lg_ref[h]
    c = RET_CHUNK
    half = RET_HEAD_DIM // 2
    row = lax.broadcasted_iota(jnp.int32, (c, c), 0)
    col = lax.broadcasted_iota(jnp.int32, (c, c), 1)
    diff = (row - col).astype(F32)
    intra = jnp.where(diff >= 0, jnp.exp(jnp.maximum(diff, 0.0) * lg), 0.0)
    idx = lax.broadcasted_iota(jnp.int32, (c, 1), 0).astype(F32)
    q_dec = jnp.exp((idx + 1.0) * lg)
    k_dec = jnp.exp((c - 1.0 - idx) * lg)
    chunk_dec = jnp.exp(jnp.full((1, 1), float(c), F32) * lg)
    gain = gain_ref[...]
    state_ref[...] = jnp.zeros_like(state_ref)

    def rope(x, cos, sin):
        x1, x2 = x[:, :half], x[:, half:]
        return jnp.concatenate([x1 * cos - x2 * sin, x2 * cos + x1 * sin], axis=1)

    def body(n, carry):
        rows = pl.ds(pl.multiple_of(n * c, c), c)
        cos = cos_ref[rows, :]
        sin = sin_ref[rows, :]
        qc = rope(q_ref[rows, :].astype(F32), cos, sin)
        kc = rope(k_ref[rows, :].astype(F32), cos, sin) * (RET_HEAD_DIM ** -0.5)
        vc = v_ref[rows, :]
        scores = _dot_nt(qc.astype(BF16), kc.astype(BF16)) * intra
        o = _dot(scores.astype(BF16), vc)
        state = state_ref[...]
        o = o + _dot((qc * q_dec).astype(BF16), state.astype(BF16))
        state_ref[...] = state * chunk_dec + _dot_tn((kc * k_dec).astype(BF16), vc)
        ms = jnp.mean(o * o, axis=-1, keepdims=True)
        y = o * lax.rsqrt(ms + EPS) * gain
        o_ref[rows, :] = (_silu(g_ref[rows, :].astype(F32)) * y).astype(o_ref.dtype)
        return carry

    lax.fori_loop(0, seq // c, body, 0, unroll=True)


def retention_mixer(proj, cos, sin, log_g, gain, batch, seq):
    hd = RET_HEAD_DIM
    nh = RET_HEADS

    def col_spec(offset):
        return pl.BlockSpec((seq, hd), lambda b, h, lg: (b, offset + h))

    table_spec = pl.BlockSpec((None, seq, LANES), lambda b, h, lg: (b, 0, 0))
    grid_spec = pltpu.PrefetchScalarGridSpec(
        num_scalar_prefetch=1,
        grid=(batch, nh),
        in_specs=[col_spec(0), col_spec(nh), col_spec(2 * nh), col_spec(3 * nh),
                  table_spec, table_spec,
                  pl.BlockSpec((None, 1, hd), lambda b, h, lg: (h, 0, 0))],
        out_specs=pl.BlockSpec((seq, hd), lambda b, h, lg: (b, h)),
        scratch_shapes=[pltpu.VMEM((hd, hd), F32)],
    )
    return pl.pallas_call(
        functools.partial(_retention_kernel, seq=seq),
        grid_spec=grid_spec,
        out_shape=jax.ShapeDtypeStruct((batch * seq, TOKEN_WIDTH), BF16),
        compiler_params=_params(("parallel", "parallel")),
        name="retention",
    )(log_g, proj, proj, proj, proj, cos, sin, gain.reshape(nh, 1, hd))


def _swa_kernel(sink_ref, q_ref, kc_ref, kp_ref, vc_ref, vp_ref, cosc_ref, sinc_ref,
                cosp_ref, sinp_ref, qg_ref, kg_ref, o_ref):
    n = pl.program_id(1)
    w = WINDOW
    dh = SWA_HEAD_DIM
    lane = lax.broadcasted_iota(jnp.int32, (1, LANES), 1)
    low_half = lane < dh
    first_rot = (lane % dh) < (dh // 2)
    r = lax.broadcasted_iota(jnp.int32, (LANES, LANES), 0) // dh
    cidx = lax.broadcasted_iota(jnp.int32, (LANES, LANES), 1) // dh
    ones_bd = (r == cidx).astype(BF16)

    def norm_rope(x, gain, cos, sin):
        ss = x * x
        hi = ss.astype(BF16)
        lo = (ss - hi.astype(F32)).astype(BF16)
        gs = _dot(hi, ones_bd) + _dot(lo, ones_bd)
        y = x * lax.rsqrt(gs * (1.0 / dh) + EPS) * gain
        rot = jnp.where(first_rot, pltpu.roll(y, LANES - dh // 2, 1), pltpu.roll(y, dh // 2, 1))
        return y * cos + rot * sin

    cos_c, sin_c = cosc_ref[...], sinc_ref[...]
    cos_p, sin_p = cosp_ref[...], sinp_ref[...]
    scale = dh ** -0.5
    qg, kg = qg_ref[...] * scale, kg_ref[...]

    i_idx = lax.broadcasted_iota(jnp.int32, (w, 2 * w), 0)
    j_idx = lax.broadcasted_iota(jnp.int32, (w, 2 * w), 1)
    valid = (j_idx > i_idx) & (j_idx <= i_idx + w) & ((n > 0) | (j_idx >= w))
    bias = jnp.where(valid, 0.0, jnp.finfo(F32).min)
    bias = jnp.concatenate([bias] * SWA_GROUP, axis=0)
    sink_col = lane == 0
    first_key = lax.broadcasted_iota(jnp.int32, (2 * w, 1), 0) == 0
    e_row = lax.broadcasted_iota(jnp.int32, (4 * w, LANES), 0) < 2 * w
    e_lane = lax.broadcasted_iota(jnp.int32, (4 * w, LANES), 1) < dh
    e_mat = (e_row == e_lane).astype(BF16)

    n_kv_slabs = SWA_KV_HEADS * dh // LANES
    k_slabs, v_slabs = [], []
    for s in range(n_kv_slabs):
        cols = slice(s * LANES, (s + 1) * LANES)
        k_cur = norm_rope(kc_ref[:, cols].astype(F32), kg, cos_c, sin_c)
        k_prev = norm_rope(kp_ref[:, cols].astype(F32), kg, cos_p, sin_p)
        k_slabs.append(jnp.concatenate([k_prev, k_cur], axis=0))
        v_slab = jnp.concatenate([vp_ref[:, cols], vc_ref[:, cols]], axis=0)
        v_slabs.append(jnp.where(first_key, jnp.zeros_like(v_slab), v_slab))

    for kv in range(SWA_KV_HEADS):
        k_slab = k_slabs[kv // 2]
        v_slab = v_slabs[kv // 2]
        if kv % 2 == 0:
            k_own = jnp.where(low_half, k_slab, 0.0)
            v_lo = jnp.where(low_half, v_slab, jnp.zeros_like(v_slab))
            v_hi = pltpu.roll(v_lo.astype(F32), dh, 1).astype(BF16)
        else:
            k_own = jnp.where(low_half, 0.0, k_slab)
            v_hi = jnp.where(low_half, jnp.zeros_like(v_slab), v_slab)
            v_lo = pltpu.roll(v_hi.astype(F32), dh, 1).astype(BF16)
        kk = (k_own + pltpu.roll(k_own, dh, 1)).astype(BF16)
        v_ext = jnp.concatenate([jnp.concatenate([v_lo, v_hi], axis=0), e_mat], axis=1)

        n_pair = SWA_GROUP // 2
        q_rows = []
        for pair in range(n_pair):
            col0 = (kv * n_pair + pair) * LANES
            qn = norm_rope(q_ref[:, col0:col0 + LANES].astype(F32), qg, cos_c, sin_c)
            q_rows += [jnp.where(low_half, qn, 0.0), jnp.where(low_half, 0.0, qn)]
        q8 = jnp.concatenate(q_rows, axis=0).astype(BF16)
        s = _dot_nt(q8, kk) + bias
        s_first = jnp.concatenate(
            [jnp.where(sink_col, sink_ref[kv * SWA_GROUP + g], s[g * w:(g + 1) * w, :LANES])
             for g in range(SWA_GROUP)], axis=0)
        s = jnp.concatenate([s_first, s[:, LANES:]], axis=1)
        p = jnp.exp(s - jnp.max(s, axis=-1, keepdims=True)).astype(BF16)
        p_pairs = jnp.concatenate(
            [jnp.concatenate([p[(2 * i) * w:(2 * i + 1) * w], p[(2 * i + 1) * w:(2 * i + 2) * w]], axis=1)
             for i in range(n_pair)], axis=0)
        ov = _dot(p_pairs, v_ext)
        out = ov[:, :LANES] / ov[:, LANES:]
        for pair in range(n_pair):
            col0 = (kv * n_pair + pair) * LANES
            o_ref[:, col0:col0 + LANES] = out[pair * w:(pair + 1) * w].astype(o_ref.dtype)


def swa_mixer(proj, cos, sin, q_gain, k_gain, sinks, batch, seq):
    w = WINDOW
    nb = seq // w
    kv_w = SWA_KV_HEADS * SWA_HEAD_DIM
    k_blk = TOKEN_WIDTH // kv_w

    def cur(b, n, s):
        return b * nb + n

    def prev(b, n, s):
        return b * nb + jnp.maximum(n - 1, 0)

    tab_cur = pl.BlockSpec((None, w, LANES), lambda b, n, s: (b, n, 0))
    tab_prev = pl.BlockSpec((None, w, LANES), lambda b, n, s: (b, jnp.maximum(n - 1, 0), 0))
    gain_spec = pl.BlockSpec((1, LANES), lambda b, n, s: (0, 0))
    grid_spec = pltpu.PrefetchScalarGridSpec(
        num_scalar_prefetch=1,
        grid=(batch, nb),
        in_specs=[pl.BlockSpec((w, TOKEN_WIDTH), lambda b, n, s: (cur(b, n, s), 0)),
                  pl.BlockSpec((w, kv_w), lambda b, n, s: (cur(b, n, s), k_blk)),
                  pl.BlockSpec((w, kv_w), lambda b, n, s: (prev(b, n, s), k_blk)),
                  pl.BlockSpec((w, kv_w), lambda b, n, s: (cur(b, n, s), k_blk + 1)),
                  pl.BlockSpec((w, kv_w), lambda b, n, s: (prev(b, n, s), k_blk + 1)),
                  tab_cur, tab_cur, tab_prev, tab_prev, gain_spec, gain_spec],
        out_specs=pl.BlockSpec((w, TOKEN_WIDTH), lambda b, n, s: (cur(b, n, s), 0)),
    )
    tile = LANES // SWA_HEAD_DIM
    return pl.pallas_call(
        _swa_kernel,
        grid_spec=grid_spec,
        out_shape=jax.ShapeDtypeStruct((batch * seq, TOKEN_WIDTH), BF16),
        compiler_params=_params(("parallel", "parallel")),
        name="swa",
    )(sinks, proj, proj, proj, proj, proj, cos, sin, cos, sin,
      jnp.tile(q_gain, tile).reshape(1, LANES), jnp.tile(k_gain, tile).reshape(1, LANES))


def _mem_attn_kernel(*refs):
    q_refs = refs[:MEM_HEADS]
    mk_ref, mv_ref, qg_ref, kg_ref, o_ref = refs[MEM_HEADS:]
    hd = MEM_HEAD_DIM

    def norm(x, g):
        ms = jnp.mean(x * x, axis=-1, keepdims=True)
        return (x * lax.rsqrt(ms + EPS) * g).astype(BF16)

    for h in range(MEM_HEADS):
        cols = slice(h * hd, (h + 1) * hd)
        q = norm(q_refs[h][...].astype(F32), qg_ref[...])
        k = norm(mk_ref[:, cols], kg_ref[...])
        s = _dot_nt(q, k) * (hd ** -0.5)
        p = jnp.exp(s - jnp.max(s, axis=-1, keepdims=True))
        p = (p / jnp.sum(p, axis=-1, keepdims=True)).astype(BF16)
        o_ref[:, cols] = _dot(p, mv_ref[:, cols].astype(BF16)).astype(o_ref.dtype)


def memory_attention(proj, q_col_block, mkv, q_gain, k_gain, batch, seq, block_rows=1024):
    hd = MEM_HEAD_DIM
    nblk = seq // block_rows
    gain_spec = pl.BlockSpec((1, hd), lambda b, i: (0, 0))
    q_specs = [pl.BlockSpec((block_rows, hd),
                            functools.partial(lambda b, i, h: (b * nblk + i, q_col_block + h), h=h))
               for h in range(MEM_HEADS)]
    return pl.pallas_call(
        _mem_attn_kernel,
        grid=(batch, nblk),
        in_specs=q_specs + [pl.BlockSpec((N_MEM, MEM_WIDTH), lambda b, i: (b, 0)),
                            pl.BlockSpec((N_MEM, MEM_WIDTH), lambda b, i: (b, 1)),
                            gain_spec, gain_spec],
        out_specs=pl.BlockSpec((block_rows, MEM_WIDTH), lambda b, i: (b * nblk + i, 0)),
        out_shape=jax.ShapeDtypeStruct((batch * seq, MEM_WIDTH), BF16),
        compiler_params=_params(("parallel", "parallel")),
        name="mem_attn",
    )(*([proj] * MEM_HEADS), mkv, mkv, q_gain.reshape(1, hd), k_gain.reshape(1, hd))


def _norm_operands(x_new, gain, o_ref, xg_ref, rstd_ref, ssq_ref, first, last):
    o_ref[...] = x_new
    xg_ref[...] = (x_new * gain).astype(xg_ref.dtype)
    part = jnp.sum(x_new * x_new, axis=-1, keepdims=True)

    @pl.when(first)
    def _():
        ssq_ref[...] = part

    @pl.when(jnp.logical_not(first))
    def _():
        ssq_ref[...] += part

    @pl.when(last)
    def _():
        rstd_ref[...] = lax.rsqrt(ssq_ref[...] * (1.0 / D_MODEL) + EPS)


def _out_proj_kernel(tok_ref, mem_ref, w_ref, x_ref, g_ref, o_ref, xg_ref, rstd_ref, ssq_ref):
    j = pl.program_id(1)
    acc = _dot(tok_ref[...], w_ref[:TOKEN_WIDTH, :])
    acc = acc + _dot(mem_ref[...], w_ref[TOKEN_WIDTH:, :])
    _norm_operands(x_ref[...] + acc, g_ref[...], o_ref, xg_ref, rstd_ref, ssq_ref,
                   j == 0, j == pl.num_programs(1) - 1)


def out_projection(tok, mem, w_o, x, next_gain):
    t = x.shape[0]
    bm, bn = 1024, 512
    return pl.pallas_call(
        _out_proj_kernel,
        grid=(t // bm, D_MODEL // bn),
        in_specs=[pl.BlockSpec((bm, TOKEN_WIDTH), lambda i, j: (i, 0)),
                  pl.BlockSpec((bm, MEM_WIDTH), lambda i, j: (i, 0)),
                  pl.BlockSpec((D_MODEL, bn), lambda i, j: (0, j)),
                  pl.BlockSpec((bm, bn), lambda i, j: (i, j)),
                  pl.BlockSpec((1, bn), lambda i, j: (0, j))],
        out_specs=[pl.BlockSpec((bm, bn), lambda i, j: (i, j)),
                   pl.BlockSpec((bm, bn), lambda i, j: (i, j)),
                   pl.BlockSpec((bm, 1), lambda i, j: (i, 0))],
        out_shape=[jax.ShapeDtypeStruct((t, D_MODEL), F32),
                   jax.ShapeDtypeStruct((t, D_MODEL), BF16),
                   jax.ShapeDtypeStruct((t, 1), F32)],
        scratch_shapes=[pltpu.VMEM((bm, 1), F32)],
        compiler_params=_params(("parallel", "arbitrary")),
        name="out_proj",
    )(tok, mem, w_o, x, next_gain.reshape(1, D_MODEL))


FFN_BM = 1024
FFN_RM = 512
FFN_FC = 256
FFN_NC = 256
FFN_VMEM_LIMIT = 62 * 1024 * 1024


def _ffn_kernel(xg_ref, rstd_ref, wg_ref, wu_ref, wd_ref, x_ref, *rest, n_f, emit_norm):
    if emit_norm:
        g_ref, o_ref, xg_out_ref, rstd_out_ref, acc_ref, hid_ref, ssq_ref = rest
    else:
        o_ref, acc_ref, hid_ref = rest
    f = pl.program_id(1)
    n_slab = D_MODEL // FFN_NC

    @pl.when(f == 0)
    def _():
        acc_ref[...] = jnp.zeros_like(acc_ref)

    @pl.when(f < n_slab)
    def _():
        acc_ref[f] += x_ref[...]

    @pl.when(f < n_f)
    def _():
        wg = wg_ref[...].astype(BF16)
        wu = wu_ref[...].astype(BF16)
        for r in range(FFN_BM // FFN_RM):
            rows = slice(r * FFN_RM, (r + 1) * FFN_RM)
            xg = xg_ref[rows, :]
            rstd = rstd_ref[rows, :]
            gate = _dot(xg, wg) * rstd
            up = _dot(xg, wu) * rstd
            hid_ref[rows, :] = (_silu(gate) * up).astype(BF16)
        hid = hid_ref[...]
        for c in range(n_slab):
            acc_ref[c] += _dot(hid, wd_ref[:, c * FFN_NC:(c + 1) * FFN_NC].astype(BF16))

    @pl.when(f >= n_f)
    def _():
        x_new = acc_ref[f - n_f]
        if emit_norm:
            _norm_operands(x_new, g_ref[...], o_ref, xg_out_ref, rstd_out_ref, ssq_ref,
                           f == n_f, f == pl.num_programs(1) - 1)
        else:
            o_ref[...] = x_new


def ffn(xg, rstd, w_gate_up, w_down, x, next_gain=None):
    t = x.shape[0]
    n_f = D_FF // FFN_FC
    n_slab = D_MODEL // FFN_NC
    last = n_f - 1
    emit_norm = next_gain is not None

    def fidx(f):
        return jnp.minimum(f, last)

    def oidx(f):
        return jnp.maximum(f - n_f, 0)

    slab_spec = pl.BlockSpec((FFN_BM, FFN_NC), lambda i, f: (i, oidx(f)))
    x_spec = pl.BlockSpec((FFN_BM, FFN_NC), lambda i, f: (i, jnp.minimum(f, n_slab - 1)))
    in_specs = [pl.BlockSpec((FFN_BM, D_MODEL), lambda i, f: (i, 0), pipeline_mode=pl.Buffered(1)),
                pl.BlockSpec((FFN_BM, 1), lambda i, f: (i, 0)),
                pl.BlockSpec((D_MODEL, FFN_FC), lambda i, f: (0, fidx(f))),
                pl.BlockSpec((D_MODEL, FFN_FC), lambda i, f: (0, n_f + fidx(f))),
                pl.BlockSpec((FFN_FC, D_MODEL), lambda i, f: (fidx(f), 0)),
                x_spec]
    args = [xg, rstd, w_gate_up, w_gate_up, w_down, x]
    out_specs = [slab_spec]
    out_shape = [jax.ShapeDtypeStruct((t, D_MODEL), F32)]
    scratch = [pltpu.VMEM((n_slab, FFN_BM, FFN_NC), F32), pltpu.VMEM((FFN_BM, FFN_FC), BF16)]
    if emit_norm:
        in_specs.append(pl.BlockSpec((1, FFN_NC), lambda i, f: (0, oidx(f))))
        args.append(next_gain.reshape(1, D_MODEL))
        out_specs += [slab_spec, pl.BlockSpec((FFN_BM, 1), lambda i, f: (i, 0))]
        out_shape += [jax.ShapeDtypeStruct((t, D_MODEL), BF16), jax.ShapeDtypeStruct((t, 1), F32)]
        scratch.append(pltpu.VMEM((FFN_BM, 1), F32))
    out = pl.pallas_call(
        functools.partial(_ffn_kernel, n_f=n_f, emit_norm=emit_norm),
        grid=(t // FFN_BM, n_f + n_slab),
        in_specs=in_specs,
        out_specs=out_specs,
        out_shape=out_shape,
        scratch_shapes=scratch,
        compiler_params=_params(("parallel", "arbitrary"), FFN_VMEM_LIMIT),
        name="ffn",
    )(*args)
    return out if emit_norm else out[0]


def _rope_consts(head_dim):
    half = head_dim // 2
    inv_freq = ROPE_THETA ** (-jnp.arange(0, head_dim, 2, dtype=F32) / head_dim)
    reps = LANES // half if half < LANES else 1
    freq = jnp.tile(inv_freq, reps)[:LANES]
    if half >= LANES:
        sign = jnp.ones((LANES,), F32)
    else:
        sign = jnp.tile(jnp.concatenate([-jnp.ones((half,), F32), jnp.ones((half,), F32)]), reps // 2)
    return freq.reshape(1, LANES), sign.reshape(1, LANES)


def kernel(x, mem, positions, mem_norm_g, w_mem_kv, mem_k_norm_g, l0_attn_norm_g, l0_w_in, l0_ret_norm_g, l0_mem_q_norm_g, l0_w_o, l0_ffn_norm_g, l0_w_gate_up, l0_w_down, l1_attn_norm_g, l1_w_in, l1_q_norm_g, l1_k_norm_g, l1_sinks, l1_mem_q_norm_g, l1_w_o, l1_ffn_norm_g, l1_w_gate_up, l1_w_down):
    batch, seq, d = x.shape
    t = batch * seq
    x2 = x.reshape(t, d)

    pos_f = positions.astype(F32).reshape(batch, seq, 1)
    cos_r, sin_r = rope_tables(pos_f, *_rope_consts(RET_HEAD_DIM))
    cos_s, sin_s = rope_tables(pos_f, *_rope_consts(SWA_HEAD_DIM))
    log_g = jnp.log1p(-jnp.exp2(-5.0 - jnp.arange(RET_HEADS, dtype=F32)))

    mem_n = rmsnorm_rows(mem.reshape(batch * N_MEM, d), mem_norm_g)
    mkv = matmul(mem_n, w_mem_kv, F32)

    h = rmsnorm_rows(x2, l0_attn_norm_g)
    proj = matmul(h, l0_w_in, BF16)
    tok = retention_mixer(proj, cos_r, sin_r, log_g, l0_ret_norm_g, batch, seq)
    mo = memory_attention(proj, 4 * TOKEN_WIDTH // MEM_HEAD_DIM, mkv, l0_mem_q_norm_g, mem_k_norm_g,
                          batch, seq)
    x2, xg, rstd = out_projection(tok, mo, l0_w_o.astype(BF16), x2, l0_ffn_norm_g)
    x2, xg, rstd = ffn(xg, rstd, l0_w_gate_up, l0_w_down, x2, l1_attn_norm_g)

    proj = matmul(xg, l1_w_in, BF16, rstd)
    tok = swa_mixer(proj, cos_s, sin_s, l1_q_norm_g, l1_k_norm_g, l1_sinks, batch, seq)
    q_col = (TOKEN_WIDTH + 2 * SWA_KV_HEADS * SWA_HEAD_DIM) // MEM_HEAD_DIM
    mo = memory_attention(proj, q_col, mkv, l1_mem_q_norm_g, mem_k_norm_g, batch, seq)
    x2, xg, rstd = out_projection(tok, mo, l1_w_o.astype(BF16), x2, l1_ffn_norm_g)
    x2 = ffn(xg, rstd, l1_w_gate_up, l1_w_down, x2)
    return x2.reshape(batch, seq, d)
```

```python
import functools

import jax
import jax.numpy as jnp
from jax import lax
from jax.experimental import pallas as pl
from jax.experimental.pallas import tpu as pltpu

D_MODEL = 4096
N_MEM = 256
TOKEN_WIDTH = 3 * D_MODEL // 4
MEM_WIDTH = D_MODEL // 4
MEM_HEADS = 4
MEM_HEAD_DIM = MEM_WIDTH // MEM_HEADS
RET_HEAD_DIM = 256
RET_HEADS = TOKEN_WIDTH // RET_HEAD_DIM
RET_CHUNK = 128
SWA_HEAD_DIM = 64
SWA_HEADS = TOKEN_WIDTH // SWA_HEAD_DIM
SWA_GROUP = 8
SWA_KV_HEADS = SWA_HEADS // SWA_GROUP
WINDOW = 128
D_FF = ((8 * D_MODEL + 3 * 256 - 1) // (3 * 256)) * 256
ROPE_THETA = 10000.0
EPS = 1e-6

LANES = 128
VMEM_LIMIT = 56 * 1024 * 1024

BF16 = jnp.bfloat16
F32 = jnp.float32


def _params(semantics, vmem=VMEM_LIMIT):
    return pltpu.CompilerParams(dimension_semantics=semantics, vmem_limit_bytes=vmem)


def _dot(a, b):
    return jnp.dot(a, b, preferred_element_type=F32)


def _dot_nt(a, b):
    return lax.dot_general(a, b, (((1,), (1,)), ((), ())), preferred_element_type=F32)


def _dot_tn(a, b):
    return lax.dot_general(a, b, (((0,), (0,)), ((), ())), preferred_element_type=F32)


def _silu(x):
    return x / (1.0 + jnp.exp(-x))


def _rmsnorm_kernel(x_ref, g_ref, o_ref):
    x = x_ref[...]
    ms = jnp.mean(x * x, axis=-1, keepdims=True)
    o_ref[...] = (x * lax.rsqrt(ms + EPS) * g_ref[...]).astype(o_ref.dtype)


def rmsnorm_rows(x, g, block_rows=512):
    t, d = x.shape
    return pl.pallas_call(
        _rmsnorm_kernel,
        grid=(t // block_rows,),
        in_specs=[pl.BlockSpec((block_rows, d), lambda i: (i, 0)),
                  pl.BlockSpec((1, d), lambda i: (0, 0))],
        out_specs=pl.BlockSpec((block_rows, d), lambda i: (i, 0)),
        out_shape=jax.ShapeDtypeStruct((t, d), BF16),
        compiler_params=_params(("parallel",)),
        name="rmsnorm",
    )(x, g.reshape(1, d))


def _matmul_kernel(x_ref, w_ref, o_ref):
    o_ref[...] = _dot(x_ref[...], w_ref[...].astype(BF16)).astype(o_ref.dtype)


def _scaled_matmul_kernel(x_ref, w_ref, r_ref, o_ref):
    o_ref[...] = (_dot(x_ref[...], w_ref[...].astype(BF16)) * r_ref[...]).astype(o_ref.dtype)


def _pick_block(n, candidates):
    for c in candidates:
        if n % c == 0:
            return c
    raise ValueError(f"no block size for {n}")


def matmul(x, w, out_dtype, row_scale=None):
    m, k = x.shape
    n = w.shape[1]
    bm = _pick_block(m, (2048, 1024))
    bn = _pick_block(n, (512, 256))
    in_specs = [pl.BlockSpec((bm, k), lambda i, j: (i, 0), pipeline_mode=pl.Buffered(1)),
                pl.BlockSpec((k, bn), lambda i, j: (0, j))]
    args = [x, w]
    if row_scale is not None:
        in_specs.append(pl.BlockSpec((bm, 1), lambda i, j: (i, 0)))
        args.append(row_scale)
    return pl.pallas_call(
        _matmul_kernel if row_scale is None else _scaled_matmul_kernel,
        grid=(m // bm, n // bn),
        in_specs=in_specs,
        out_specs=pl.BlockSpec((bm, bn), lambda i, j: (i, j)),
        out_shape=jax.ShapeDtypeStruct((m, n), out_dtype),
        compiler_params=_params(("parallel", "arbitrary")),
        name="matmul",
    )(*args)


def _rope_table_kernel(pos_ref, freq_ref, sign_ref, cos_ref, sin_ref):
    ang = pos_ref[...] * freq_ref[...]
    cos_ref[...] = jnp.cos(ang)
    sin_ref[...] = jnp.sin(ang) * sign_ref[...]


def rope_tables(pos_f, freq, sign):
    b, s, _ = pos_f.shape
    out = jax.ShapeDtypeStruct((b, s, LANES), F32)
    return pl.pallas_call(
        _rope_table_kernel,
        grid=(b,),
        in_specs=[pl.BlockSpec((None, s, 1), lambda i: (i, 0, 0)),
                  pl.BlockSpec((1, LANES), lambda i: (0, 0)),
                  pl.BlockSpec((1, LANES), lambda i: (0, 0))],
        out_specs=[pl.BlockSpec((None, s, LANES), lambda i: (i, 0, 0))] * 2,
        out_shape=[out, out],
        compiler_params=_params(("parallel",)),
        name="rope_tables",
    )(pos_f, freq, sign)


def _retention_kernel(lg_ref, q_ref, k_ref, v_ref, g_ref, cos_ref, sin_ref, gain_ref,
                      o_ref, state_ref, *, seq):
    h = pl.program_id(1)
    lg = lg_ref[h]
    c = RET_CHUNK
    half = RET_HEAD_DIM // 2
    row = lax.broadcasted_iota(jnp.int32, (c, c), 0)
    col = lax.broadcasted_iota(jnp.int32, (c, c), 1)
    diff = (row - col).astype(F32)
    k_scale = RET_HEAD_DIM ** -0.5
    intra = jnp.where(diff >= 0, jnp.exp(jnp.maximum(diff, 0.0) * lg), 0.0) * k_scale
    idx = lax.broadcasted_iota(jnp.int32, (c, 1), 0).astype(F32)
    q_dec = jnp.exp((idx + 1.0) * lg)
    k_dec = jnp.exp((c - 1.0 - idx) * lg) * k_scale
    chunk_dec = jnp.exp(jnp.full((1, 1), float(c), F32) * lg)
    gain = gain_ref[...]
    state_ref[...] = jnp.zeros_like(state_ref)

    def rope(x, cos, sin):
        x1, x2 = x[:, :half], x[:, half:]
        return jnp.concatenate([x1 * cos - x2 * sin, x2 * cos + x1 * sin], axis=1)

    def body(n, carry):
        rows = pl.ds(pl.multiple_of(n * c, c), c)
        cos = cos_ref[rows, :]
        sin = sin_ref[rows, :]
        qc = rope(q_ref[rows, :].astype(F32), cos, sin)
        kc = rope(k_ref[rows, :].astype(F32), cos, sin)
        vc = v_ref[rows, :]
        scores = _dot_nt(qc.astype(BF16), kc.astype(BF16)) * intra
        o = _dot(scores.astype(BF16), vc)
        state = state_ref[...]
        o = o + _dot((qc * q_dec).astype(BF16), state.astype(BF16))
        state_ref[...] = state * chunk_dec + _dot_tn((kc * k_dec).astype(BF16), vc)
        ms = jnp.mean(o * o, axis=-1, keepdims=True)
        y = o * lax.rsqrt(ms + EPS) * gain
        o_ref[rows, :] = (_silu(g_ref[rows, :].astype(F32)) * y).astype(o_ref.dtype)
        return carry

    lax.fori_loop(0, seq // c, body, 0, unroll=True)


def retention_mixer(proj, cos, sin, log_g, gain, batch, seq):
    hd = RET_HEAD_DIM
    nh = RET_HEADS

    def col_spec(offset):
        return pl.BlockSpec((seq, hd), lambda b, h, lg: (b, offset + h))

    table_spec = pl.BlockSpec((None, seq, LANES), lambda b, h, lg: (b, 0, 0))
    grid_spec = pltpu.PrefetchScalarGridSpec(
        num_scalar_prefetch=1,
        grid=(batch, nh),
        in_specs=[col_spec(0), col_spec(nh), col_spec(2 * nh), col_spec(3 * nh),
                  table_spec, table_spec,
                  pl.BlockSpec((None, 1, hd), lambda b, h, lg: (h, 0, 0))],
        out_specs=pl.BlockSpec((seq, hd), lambda b, h, lg: (b, h)),
        scratch_shapes=[pltpu.VMEM((hd, hd), F32)],
    )
    return pl.pallas_call(
        functools.partial(_retention_kernel, seq=seq),
        grid_spec=grid_spec,
        out_shape=jax.ShapeDtypeStruct((batch * seq, TOKEN_WIDTH), BF16),
        compiler_params=_params(("parallel", "parallel")),
        name="retention",
    )(log_g, proj, proj, proj, proj, cos, sin, gain.reshape(nh, 1, hd))


SWA_BLOCKS_PER_STEP = 2


def _swa_kernel(sink_ref, q_ref, kc_ref, kp_ref, vc_ref, vp_ref, cosc_ref, sinc_ref,
                cosp_ref, sinp_ref, qg_ref, kg_ref, o_ref):
    n = pl.program_id(1)
    w = WINDOW
    dh = SWA_HEAD_DIM
    nsub = SWA_BLOCKS_PER_STEP
    lane = lax.broadcasted_iota(jnp.int32, (1, LANES), 1)
    low_half = lane < dh
    first_rot = (lane % dh) < (dh // 2)
    r = lax.broadcasted_iota(jnp.int32, (LANES, LANES), 0) // dh
    cidx = lax.broadcasted_iota(jnp.int32, (LANES, LANES), 1) // dh
    ones_bd = (r == cidx).astype(BF16)

    def norm_rope(x, gain, cos, sin):
        ss = x * x
        hi = ss.astype(BF16)
        lo = (ss - hi.astype(F32)).astype(BF16)
        gs = _dot(hi, ones_bd) + _dot(lo, ones_bd)
        y = x * lax.rsqrt(gs * (1.0 / dh) + EPS) * gain
        rot = jnp.where(first_rot, pltpu.roll(y, LANES - dh // 2, 1), pltpu.roll(y, dh // 2, 1))
        return y * cos + rot * sin

    cos_all = jnp.concatenate([cosp_ref[...], cosc_ref[...]], axis=0)
    sin_all = jnp.concatenate([sinp_ref[...], sinc_ref[...]], axis=0)
    scale = dh ** -0.5
    qg, kg = qg_ref[...] * scale, kg_ref[...]

    i_idx = lax.broadcasted_iota(jnp.int32, (w, 2 * w), 0)
    j_idx = lax.broadcasted_iota(jnp.int32, (w, 2 * w), 1)
    band = (j_idx > i_idx) & (j_idx <= i_idx + w)
    neg = jnp.finfo(F32).min
    sink_col = lane == 0
    first_key = lax.broadcasted_iota(jnp.int32, (2 * w, 1), 0) == 0
    e_row = lax.broadcasted_iota(jnp.int32, (4 * w, LANES), 0) < 2 * w
    e_lane = lax.broadcasted_iota(jnp.int32, (4 * w, LANES), 1) < dh
    e_mat = (e_row == e_lane).astype(BF16)

    n_kv_slabs = SWA_KV_HEADS * dh // LANES
    k_slabs, v_slabs = [], []
    for s in range(n_kv_slabs):
        cols = slice(s * LANES, (s + 1) * LANES)
        k_raw = jnp.concatenate([kp_ref[:, cols], kc_ref[:, cols]], axis=0).astype(F32)
        k_slabs.append(norm_rope(k_raw, kg, cos_all, sin_all))
        v_slabs.append(jnp.concatenate([vp_ref[:, cols], vc_ref[:, cols]], axis=0))

    n_pair = SWA_GROUP // 2
    for sub in range(nsub):
        q_rows_of_step = slice(sub * w, (sub + 1) * w)
        kv_rows = slice(sub * w, (sub + 2) * w)
        valid = band if sub > 0 else band & ((n > 0) | (j_idx >= w))
        bias = jnp.concatenate([jnp.where(valid, 0.0, neg)] * SWA_GROUP, axis=0)
        cos_q, sin_q = cos_all[(sub + 1) * w:(sub + 2) * w], sin_all[(sub + 1) * w:(sub + 2) * w]
        for kv in range(SWA_KV_HEADS):
            k_slab = k_slabs[kv // 2][kv_rows]
            v_slab = v_slabs[kv // 2][kv_rows]
            v_slab = jnp.where(first_key, jnp.zeros_like(v_slab), v_slab)
            if kv % 2 == 0:
                k_own = jnp.where(low_half, k_slab, 0.0)
                v_lo = jnp.where(low_half, v_slab, jnp.zeros_like(v_slab))
                v_hi = pltpu.roll(v_lo.astype(F32), dh, 1).astype(BF16)
            else:
                k_own = jnp.where(low_half, 0.0, k_slab)
                v_hi = jnp.where(low_half, jnp.zeros_like(v_slab), v_slab)
                v_lo = pltpu.roll(v_hi.astype(F32), dh, 1).astype(BF16)
            kk = (k_own + pltpu.roll(k_own, dh, 1)).astype(BF16)
            v_ext = jnp.concatenate([jnp.concatenate([v_lo, v_hi], axis=0), e_mat], axis=1)

            q_rows = []
            for pair in range(n_pair):
                col0 = (kv * n_pair + pair) * LANES
                qn = norm_rope(q_ref[q_rows_of_step, col0:col0 + LANES].astype(F32), qg, cos_q, sin_q)
                q_rows += [jnp.where(low_half, qn, 0.0), jnp.where(low_half, 0.0, qn)]
            q8 = jnp.concatenate(q_rows, axis=0).astype(BF16)
            s = _dot_nt(q8, kk) + bias
            s_first = jnp.concatenate(
                [jnp.where(sink_col, sink_ref[kv * SWA_GROUP + g], s[g * w:(g + 1) * w, :LANES])
                 for g in range(SWA_GROUP)], axis=0)
            s = jnp.concatenate([s_first, s[:, LANES:]], axis=1)
            p = jnp.exp(s - jnp.max(s, axis=-1, keepdims=True)).astype(BF16)
            p_pairs = jnp.concatenate(
                [jnp.concatenate([p[(2 * i) * w:(2 * i + 1) * w], p[(2 * i + 1) * w:(2 * i + 2) * w]], axis=1)
                 for i in range(n_pair)], axis=0)
            ov = _dot(p_pairs, v_ext)
            out = ov[:, :LANES] / ov[:, LANES:]
            for pair in range(n_pair):
                col0 = (kv * n_pair + pair) * LANES
                o_ref[q_rows_of_step, col0:col0 + LANES] = out[pair * w:(pair + 1) * w].astype(o_ref.dtype)


def swa_mixer(proj, cos, sin, q_gain, k_gain, sinks, batch, seq):
    w = WINDOW
    rows = SWA_BLOCKS_PER_STEP * w
    ns = seq // rows
    kv_w = SWA_KV_HEADS * SWA_HEAD_DIM
    k_blk = TOKEN_WIDTH // kv_w

    def cur(b, n, s):
        return b * ns + n

    def prev(b, n, s):
        return jnp.maximum((b * ns + n) * SWA_BLOCKS_PER_STEP - 1, b * ns * SWA_BLOCKS_PER_STEP)

    tab_cur = pl.BlockSpec((None, rows, LANES), lambda b, n, s: (b, n, 0))
    tab_prev = pl.BlockSpec((None, w, LANES),
                            lambda b, n, s: (b, jnp.maximum(n * SWA_BLOCKS_PER_STEP - 1, 0), 0))
    gain_spec = pl.BlockSpec((1, LANES), lambda b, n, s: (0, 0))
    grid_spec = pltpu.PrefetchScalarGridSpec(
        num_scalar_prefetch=1,
        grid=(batch, ns),
        in_specs=[pl.BlockSpec((rows, TOKEN_WIDTH), lambda b, n, s: (cur(b, n, s), 0)),
                  pl.BlockSpec((rows, kv_w), lambda b, n, s: (cur(b, n, s), k_blk)),
                  pl.BlockSpec((w, kv_w), lambda b, n, s: (prev(b, n, s), k_blk)),
                  pl.BlockSpec((rows, kv_w), lambda b, n, s: (cur(b, n, s), k_blk + 1)),
                  pl.BlockSpec((w, kv_w), lambda b, n, s: (prev(b, n, s), k_blk + 1)),
                  tab_cur, tab_cur, tab_prev, tab_prev, gain_spec, gain_spec],
        out_specs=pl.BlockSpec((rows, TOKEN_WIDTH), lambda b, n, s: (cur(b, n, s), 0)),
    )
    tile = LANES // SWA_HEAD_DIM
    return pl.pallas_call(
        _swa_kernel,
        grid_spec=grid_spec,
        out_shape=jax.ShapeDtypeStruct((batch * seq, TOKEN_WIDTH), BF16),
        compiler_params=_params(("parallel", "parallel")),
        name="swa",
    )(sinks, proj, proj, proj, proj, proj, cos, sin, cos, sin,
      jnp.tile(q_gain, tile).reshape(1, LANES), jnp.tile(k_gain, tile).reshape(1, LANES))


def _mem_attn_kernel(*refs):
    q_refs = refs[:MEM_HEADS]
    mk_ref, mv_ref, qg_ref, kg_ref, w_ref, o_ref, w16_ref = refs[MEM_HEADS:]
    hd = MEM_HEAD_DIM
    w16_ref[...] = w_ref[...].astype(w16_ref.dtype)

    def norm(x, g):
        ms = jnp.mean(x * x, axis=-1, keepdims=True)
        return (x * lax.rsqrt(ms + EPS) * g).astype(BF16)

    for h in range(MEM_HEADS):
        cols = slice(h * hd, (h + 1) * hd)
        q = norm(q_refs[h][...].astype(F32), qg_ref[...])
        k = norm(mk_ref[:, cols], kg_ref[...])
        s = _dot_nt(q, k) * (hd ** -0.5)
        p = jnp.exp(s - jnp.max(s, axis=-1, keepdims=True))
        p = (p / jnp.sum(p, axis=-1, keepdims=True)).astype(BF16)
        o_ref[:, cols] = _dot(p, mv_ref[:, cols].astype(BF16)).astype(o_ref.dtype)


def memory_attention(proj, q_col_block, mkv, q_gain, k_gain, w_o, batch, seq, block_rows=1024):
    hd = MEM_HEAD_DIM
    nblk = seq // block_rows
    w_rows = w_o.shape[0] // (batch * nblk)
    w_spec = pl.BlockSpec((w_rows, w_o.shape[1]), lambda b, i: (b * nblk + i, 0))
    gain_spec = pl.BlockSpec((1, hd), lambda b, i: (0, 0))
    q_specs = [pl.BlockSpec((block_rows, hd),
                            functools.partial(lambda b, i, h: (b * nblk + i, q_col_block + h), h=h))
               for h in range(MEM_HEADS)]
    return pl.pallas_call(
        _mem_attn_kernel,
        grid=(batch, nblk),
        in_specs=q_specs + [pl.BlockSpec((N_MEM, MEM_WIDTH), lambda b, i: (b, 0)),
                            pl.BlockSpec((N_MEM, MEM_WIDTH), lambda b, i: (b, 1)),
                            gain_spec, gain_spec, w_spec],
        out_specs=[pl.BlockSpec((block_rows, MEM_WIDTH), lambda b, i: (b * nblk + i, 0)), w_spec],
        out_shape=[jax.ShapeDtypeStruct((batch * seq, MEM_WIDTH), BF16),
                   jax.ShapeDtypeStruct(w_o.shape, BF16)],
        compiler_params=_params(("parallel", "parallel")),
        name="mem_attn",
    )(*([proj] * MEM_HEADS), mkv, mkv, q_gain.reshape(1, hd), k_gain.reshape(1, hd), w_o)


def _norm_operands(x_new, gain, o_ref, xg_ref, rstd_ref, ssq_ref):
    o_ref[...] = x_new
    xg_ref[...] = (x_new * gain).astype(xg_ref.dtype)
    ssq = ssq_ref[...] + jnp.sum(x_new * x_new, axis=-1, keepdims=True)
    ssq_ref[...] = ssq
    rstd_ref[...] = lax.rsqrt(ssq * (1.0 / D_MODEL) + EPS)


def _out_proj_kernel(tok_ref, mem_ref, w_ref, x_ref, g_ref, o_ref, xg_ref, rstd_ref, ssq_ref):
    @pl.when(pl.program_id(1) == 0)
    def _():
        ssq_ref[...] = jnp.zeros_like(ssq_ref)

    acc = _dot(tok_ref[...], w_ref[:TOKEN_WIDTH, :])
    acc = acc + _dot(mem_ref[...], w_ref[TOKEN_WIDTH:, :])
    _norm_operands(x_ref[...] + acc, g_ref[...], o_ref, xg_ref, rstd_ref, ssq_ref)


def out_projection(tok, mem, w_o, x, next_gain):
    t = x.shape[0]
    bm, bn = 1024, 512
    return pl.pallas_call(
        _out_proj_kernel,
        grid=(t // bm, D_MODEL // bn),
        in_specs=[pl.BlockSpec((bm, TOKEN_WIDTH), lambda i, j: (i, 0)),
                  pl.BlockSpec((bm, MEM_WIDTH), lambda i, j: (i, 0)),
                  pl.BlockSpec((D_MODEL, bn), lambda i, j: (0, j)),
                  pl.BlockSpec((bm, bn), lambda i, j: (i, j)),
                  pl.BlockSpec((1, bn), lambda i, j: (0, j))],
        out_specs=[pl.BlockSpec((bm, bn), lambda i, j: (i, j)),
                   pl.BlockSpec((bm, bn), lambda i, j: (i, j)),
                   pl.BlockSpec((bm, 1), lambda i, j: (i, 0))],
        out_shape=[jax.ShapeDtypeStruct((t, D_MODEL), F32),
                   jax.ShapeDtypeStruct((t, D_MODEL), BF16),
                   jax.ShapeDtypeStruct((t, 1), F32)],
        scratch_shapes=[pltpu.VMEM((bm, 1), F32)],
        compiler_params=_params(("parallel", "arbitrary")),
        name="out_proj",
    )(tok, mem, w_o, x, next_gain.reshape(1, D_MODEL))


FFN_BM = 1024
FFN_RM = 512
FFN_FC = 256
FFN_NC = 256
FFN_VMEM_LIMIT = 62 * 1024 * 1024


def _ffn_kernel(xg_ref, rstd_ref, wg_ref, wu_ref, wd_ref, x_ref, *rest, n_f, emit_norm):
    if emit_norm:
        g_ref, o_ref, xg_out_ref, rstd_out_ref, acc_ref, hid_ref, ssq_ref = rest
    else:
        o_ref, acc_ref, hid_ref = rest
    f = pl.program_id(1)
    n_slab = D_MODEL // FFN_NC

    @pl.when(f == 0)
    def _():
        acc_ref[...] = jnp.zeros_like(acc_ref)
        if emit_norm:
            ssq_ref[...] = jnp.zeros_like(ssq_ref)

    @pl.when(f < n_slab)
    def _():
        acc_ref[f] += x_ref[...]

    @pl.when(f < n_f)
    def _():
        wg = wg_ref[...].astype(BF16)
        wu = wu_ref[...].astype(BF16)
        for r in range(FFN_BM // FFN_RM):
            rows = slice(r * FFN_RM, (r + 1) * FFN_RM)
            xg = xg_ref[rows, :]
            rstd = rstd_ref[rows, :]
            gate = _dot(xg, wg) * rstd
            up = _dot(xg, wu) * rstd
            hid_ref[rows, :] = (_silu(gate) * up).astype(BF16)
        hid = hid_ref[...]
        for c in range(n_slab):
            acc_ref[c] += _dot(hid, wd_ref[:, c * FFN_NC:(c + 1) * FFN_NC].astype(BF16))

    @pl.when(f >= n_f)
    def _():
        x_new = acc_ref[f - n_f]
        if emit_norm:
            _norm_operands(x_new, g_ref[...], o_ref, xg_out_ref, rstd_out_ref, ssq_ref)
        else:
            o_ref[...] = x_new


def ffn(xg, rstd, w_gate_up, w_down, x, next_gain=None):
    t = x.shape[0]
    n_f = D_FF // FFN_FC
    n_slab = D_MODEL // FFN_NC
    last = n_f - 1
    emit_norm = next_gain is not None

    def fidx(f):
        return jnp.minimum(f, last)

    def oidx(f):
        return jnp.maximum(f - n_f, 0)

    slab_spec = pl.BlockSpec((FFN_BM, FFN_NC), lambda i, f: (i, oidx(f)))
    x_spec = pl.BlockSpec((FFN_BM, FFN_NC), lambda i, f: (i, jnp.minimum(f, n_slab - 1)))
    in_specs = [pl.BlockSpec((FFN_BM, D_MODEL), lambda i, f: (i, 0), pipeline_mode=pl.Buffered(1)),
                pl.BlockSpec((FFN_BM, 1), lambda i, f: (i, 0)),
                pl.BlockSpec((D_MODEL, FFN_FC), lambda i, f: (0, fidx(f))),
                pl.BlockSpec((D_MODEL, FFN_FC), lambda i, f: (0, n_f + fidx(f))),
                pl.BlockSpec((FFN_FC, D_MODEL), lambda i, f: (fidx(f), 0)),
                x_spec]
    args = [xg, rstd, w_gate_up, w_gate_up, w_down, x]
    out_specs = [slab_spec]
    out_shape = [jax.ShapeDtypeStruct((t, D_MODEL), F32)]
    scratch = [pltpu.VMEM((n_slab, FFN_BM, FFN_NC), F32), pltpu.VMEM((FFN_BM, FFN_FC), BF16)]
    if emit_norm:
        in_specs.append(pl.BlockSpec((1, FFN_NC), lambda i, f: (0, oidx(f))))
        args.append(next_gain.reshape(1, D_MODEL))
        out_specs += [slab_spec, pl.BlockSpec((FFN_BM, 1), lambda i, f: (i, 0))]
        out_shape += [jax.ShapeDtypeStruct((t, D_MODEL), BF16), jax.ShapeDtypeStruct((t, 1), F32)]
        scratch.append(pltpu.VMEM((FFN_BM, 1), F32))
    out = pl.pallas_call(
        functools.partial(_ffn_kernel, n_f=n_f, emit_norm=emit_norm),
        grid=(t // FFN_BM, n_f + n_slab),
        in_specs=in_specs,
        out_specs=out_specs,
        out_shape=out_shape,
        scratch_shapes=scratch,
        compiler_params=_params(("parallel", "arbitrary"), FFN_VMEM_LIMIT),
        name="ffn",
    )(*args)
    return out if emit_norm else out[0]


def _rope_consts(head_dim):
    half = head_dim // 2
    inv_freq = ROPE_THETA ** (-jnp.arange(0, head_dim, 2, dtype=F32) / head_dim)
    reps = LANES // half if half < LANES else 1
    freq = jnp.tile(inv_freq, reps)[:LANES]
    if half >= LANES:
        sign = jnp.ones((LANES,), F32)
    else:
        sign = jnp.tile(jnp.concatenate([-jnp.ones((half,), F32), jnp.ones((half,), F32)]), reps // 2)
    return freq.reshape(1, LANES), sign.reshape(1, LANES)


def kernel(x, mem, positions, mem_norm_g, w_mem_kv, mem_k_norm_g, l0_attn_norm_g, l0_w_in, l0_ret_norm_g, l0_mem_q_norm_g, l0_w_o, l0_ffn_norm_g, l0_w_gate_up, l0_w_down, l1_attn_norm_g, l1_w_in, l1_q_norm_g, l1_k_norm_g, l1_sinks, l1_mem_q_norm_g, l1_w_o, l1_ffn_norm_g, l1_w_gate_up, l1_w_down):
    batch, seq, d = x.shape
    t = batch * seq
    x2 = x.reshape(t, d)

    pos_f = positions.astype(F32).reshape(batch, seq, 1)
    cos_r, sin_r = rope_tables(pos_f, *_rope_consts(RET_HEAD_DIM))
    cos_s, sin_s = rope_tables(pos_f, *_rope_consts(SWA_HEAD_DIM))
    log_g = jnp.log1p(-jnp.exp2(-5.0 - jnp.arange(RET_HEADS, dtype=F32)))

    mem_n = rmsnorm_rows(mem.reshape(batch * N_MEM, d), mem_norm_g)
    mkv = matmul(mem_n, w_mem_kv, F32)

    h = rmsnorm_rows(x2, l0_attn_norm_g)
    proj = matmul(h, l0_w_in, BF16)
    tok = retention_mixer(proj, cos_r, sin_r, log_g, l0_ret_norm_g, batch, seq)
    mo, w_o = memory_attention(proj, 4 * TOKEN_WIDTH // MEM_HEAD_DIM, mkv, l0_mem_q_norm_g, mem_k_norm_g,
                               l0_w_o, batch, seq)
    x2, xg, rstd = out_projection(tok, mo, w_o, x2, l0_ffn_norm_g)
    x2, xg, rstd = ffn(xg, rstd, l0_w_gate_up, l0_w_down, x2, l1_attn_norm_g)

    proj = matmul(xg, l1_w_in, BF16, rstd)
    tok = swa_mixer(proj, cos_s, sin_s, l1_q_norm_g, l1_k_norm_g, l1_sinks, batch, seq)
    q_col = (TOKEN_WIDTH + 2 * SWA_KV_HEADS * SWA_HEAD_DIM) // MEM_HEAD_DIM
    mo, w_o = memory_attention(proj, q_col, mkv, l1_mem_q_norm_g, mem_k_norm_g, l1_w_o, batch, seq)
    x2, xg, rstd = out_projection(tok, mo, w_o, x2, l1_ffn_norm_g)
    x2 = ffn(xg, rstd, l1_w_gate_up, l1_w_down, x2)
    return x2.reshape(batch, seq, d)
```

```python
import functools

import jax
import jax.numpy as jnp
from jax import lax
from jax.experimental import pallas as pl
from jax.experimental.pallas import tpu as pltpu

D_MODEL = 4096
N_MEM = 256
TOKEN_WIDTH = 3 * D_MODEL // 4
MEM_WIDTH = D_MODEL // 4
MEM_HEADS = 4
MEM_HEAD_DIM = MEM_WIDTH // MEM_HEADS
RET_HEAD_DIM = 256
RET_HEADS = TOKEN_WIDTH // RET_HEAD_DIM
RET_CHUNK = 128
SWA_HEAD_DIM = 64
SWA_HEADS = TOKEN_WIDTH // SWA_HEAD_DIM
SWA_GROUP = 8
SWA_KV_HEADS = SWA_HEADS // SWA_GROUP
WINDOW = 128
D_FF = ((8 * D_MODEL + 3 * 256 - 1) // (3 * 256)) * 256
ROPE_THETA = 10000.0
EPS = 1e-6

LANES = 128
VMEM_LIMIT = 56 * 1024 * 1024

BF16 = jnp.bfloat16
F32 = jnp.float32


def _params(semantics, vmem=VMEM_LIMIT):
    return pltpu.CompilerParams(dimension_semantics=semantics, vmem_limit_bytes=vmem)


def _dot(a, b):
    return jnp.dot(a, b, preferred_element_type=F32)


def _dot_nt(a, b):
    return lax.dot_general(a, b, (((1,), (1,)), ((), ())), preferred_element_type=F32)


def _dot_tn(a, b):
    return lax.dot_general(a, b, (((0,), (0,)), ((), ())), preferred_element_type=F32)


def _silu(x):
    return x / (1.0 + jnp.exp(-x))


def _rmsnorm_kernel(x_ref, g_ref, o_ref):
    x = x_ref[...]
    ms = jnp.mean(x * x, axis=-1, keepdims=True)
    o_ref[...] = (x * lax.rsqrt(ms + EPS) * g_ref[...]).astype(o_ref.dtype)


def rmsnorm_rows(x, g, block_rows=512):
    t, d = x.shape
    return pl.pallas_call(
        _rmsnorm_kernel,
        grid=(t // block_rows,),
        in_specs=[pl.BlockSpec((block_rows, d), lambda i: (i, 0)),
                  pl.BlockSpec((1, d), lambda i: (0, 0))],
        out_specs=pl.BlockSpec((block_rows, d), lambda i: (i, 0)),
        out_shape=jax.ShapeDtypeStruct((t, d), BF16),
        compiler_params=_params(("parallel",)),
        name="rmsnorm",
    )(x, g.reshape(1, d))


def _matmul_kernel(x_ref, w_ref, o_ref):
    o_ref[...] = _dot(x_ref[...], w_ref[...].astype(BF16)).astype(o_ref.dtype)


PROJ_BM = 2048
PROJ_RM = 512


def _rope_halves(x, cos, sin):
    x1, x2 = x[:, :LANES], x[:, LANES:]
    return jnp.concatenate([x1 * cos - x2 * sin, x2 * cos + x1 * sin], axis=1)


def _ret_proj_kernel(x_ref, w_ref, cos_ref, sin_ref, o_ref, *, n_rope_tiles):
    j = pl.program_id(1)

    @pl.when(j < n_rope_tiles)
    def _():
        w = w_ref[...].astype(BF16)
        for r in range(PROJ_BM // PROJ_RM):
            rows = slice(r * PROJ_RM, (r + 1) * PROJ_RM)
            y = _dot(x_ref[rows, :], w)
            cos, sin = cos_ref[rows, :], sin_ref[rows, :]
            heads = [_rope_halves(y[:, h * RET_HEAD_DIM:(h + 1) * RET_HEAD_DIM], cos, sin)
                     for h in range(y.shape[1] // RET_HEAD_DIM)]
            o_ref[rows, :] = jnp.concatenate(heads, axis=1).astype(o_ref.dtype)

    @pl.when(j >= n_rope_tiles)
    def _():
        o_ref[...] = _dot(x_ref[...], w_ref[...].astype(BF16)).astype(o_ref.dtype)


def _norm_rope_heads(x, gain, cos, sin):
    dh = SWA_HEAD_DIM
    width = x.shape[1]
    lane = lax.broadcasted_iota(jnp.int32, (1, width), 1)
    first_rot = (lane % dh) < (dh // 2)
    r = lax.broadcasted_iota(jnp.int32, (width, width), 0) // dh
    c = lax.broadcasted_iota(jnp.int32, (width, width), 1) // dh
    ones_bd = (r == c).astype(BF16)
    ss = x * x
    hi = ss.astype(BF16)
    lo = (ss - hi.astype(F32)).astype(BF16)
    gs = _dot(hi, ones_bd) + _dot(lo, ones_bd)
    y = x * lax.rsqrt(gs * (1.0 / dh) + EPS) * gain
    rot = jnp.where(first_rot, pltpu.roll(y, width - dh // 2, 1), pltpu.roll(y, dh // 2, 1))
    return y * cos + rot * sin


def _swa_proj_kernel(x_ref, w_ref, r_ref, cos_ref, sin_ref, gain_ref, flag_ref, o_ref, *, n_qk_tiles):
    j = pl.program_id(1)

    @pl.when(j < n_qk_tiles)
    def _():
        w = w_ref[...].astype(BF16)
        n_chunks = PROJ_BM // PROJ_RM
        reps = o_ref.shape[1] // LANES

        def chunk_rows(r):
            return slice(r * PROJ_RM, (r + 1) * PROJ_RM)

        def matmul_chunk(r):
            return _dot(x_ref[chunk_rows(r), :], w) * r_ref[chunk_rows(r), :]

        def epilogue(r, y):
            rows = chunk_rows(r)
            cos = jnp.concatenate([cos_ref[rows, :]] * reps, axis=1)
            sin = jnp.concatenate([sin_ref[rows, :]] * reps, axis=1)
            yn = _norm_rope_heads(y, gain_ref[...], cos, sin)
            o_ref[rows, :] = jnp.where(flag_ref[...] > 0.0, yn, y).astype(o_ref.dtype)

        y_prev = matmul_chunk(0)
        for r in range(1, n_chunks):
            y_next = matmul_chunk(r)
            epilogue(r - 1, y_prev)
            y_prev = y_next
        epilogue(n_chunks - 1, y_prev)

    @pl.when(j >= n_qk_tiles)
    def _():
        o_ref[...] = (_dot(x_ref[...], w_ref[...].astype(BF16)) * r_ref[...]).astype(o_ref.dtype)


def _pick_block(n, candidates):
    for c in candidates:
        if n % c == 0:
            return c
    raise ValueError(f"no block size for {n}")


def matmul(x, w, out_dtype):
    m, k = x.shape
    n = w.shape[1]
    bm = _pick_block(m, (2048, 1024))
    bn = _pick_block(n, (512, 256))
    return pl.pallas_call(
        _matmul_kernel,
        grid=(m // bm, n // bn),
        in_specs=[pl.BlockSpec((bm, k), lambda i, j: (i, 0), pipeline_mode=pl.Buffered(1)),
                  pl.BlockSpec((k, bn), lambda i, j: (0, j))],
        out_specs=pl.BlockSpec((bm, bn), lambda i, j: (i, j)),
        out_shape=jax.ShapeDtypeStruct((m, n), out_dtype),
        compiler_params=_params(("parallel", "arbitrary")),
        name="matmul",
    )(x, w)


def retention_projection(h, w_in, cos, sin):
    m, k = h.shape
    n = w_in.shape[1]
    bn = 512
    table_spec = pl.BlockSpec((PROJ_BM, LANES), lambda i, j: (i, 0))
    return pl.pallas_call(
        functools.partial(_ret_proj_kernel, n_rope_tiles=2 * TOKEN_WIDTH // bn),
        grid=(m // PROJ_BM, n // bn),
        in_specs=[pl.BlockSpec((PROJ_BM, k), lambda i, j: (i, 0), pipeline_mode=pl.Buffered(1)),
                  pl.BlockSpec((k, bn), lambda i, j: (0, j)),
                  table_spec, table_spec],
        out_specs=pl.BlockSpec((PROJ_BM, bn), lambda i, j: (i, j)),
        out_shape=jax.ShapeDtypeStruct((m, n), BF16),
        compiler_params=_params(("parallel", "arbitrary")),
        name="ret_proj",
    )(h, w_in, cos, sin)


def swa_projection(xg, rstd, w_in, cos, sin, q_gain, k_gain):
    m, k = xg.shape
    n = w_in.shape[1]
    bn = 256
    kv_w = SWA_KV_HEADS * SWA_HEAD_DIM
    tile = LANES // SWA_HEAD_DIM
    n_q, n_k = TOKEN_WIDTH // LANES, kv_w // LANES
    rest = n - TOKEN_WIDTH - kv_w
    gain = jnp.concatenate([jnp.tile(q_gain * (SWA_HEAD_DIM ** -0.5), tile * n_q),
                            jnp.tile(k_gain, tile * n_k), jnp.ones((rest,), F32)]).reshape(1, n)
    flag = jnp.concatenate([jnp.ones((TOKEN_WIDTH + kv_w,), F32), jnp.zeros((rest,), F32)]).reshape(1, n)
    table_spec = pl.BlockSpec((PROJ_BM, LANES), lambda i, j: (i, 0))
    col_spec = pl.BlockSpec((1, bn), lambda i, j: (0, j))
    return pl.pallas_call(
        functools.partial(_swa_proj_kernel, n_qk_tiles=-(-(TOKEN_WIDTH + kv_w) // bn)),
        grid=(m // PROJ_BM, n // bn),
        in_specs=[pl.BlockSpec((PROJ_BM, k), lambda i, j: (i, 0), pipeline_mode=pl.Buffered(1)),
                  pl.BlockSpec((k, bn), lambda i, j: (0, j)),
                  pl.BlockSpec((PROJ_BM, 1), lambda i, j: (i, 0)),
                  table_spec, table_spec, col_spec, col_spec],
        out_specs=pl.BlockSpec((PROJ_BM, bn), lambda i, j: (i, j)),
        out_shape=jax.ShapeDtypeStruct((m, n), BF16),
        compiler_params=_params(("parallel", "arbitrary")),
        name="swa_proj",
    )(xg, w_in, rstd, cos, sin, gain, flag)


def _rope_table_kernel(pos_ref, freq_ref, sign_ref, cos_ref, sin_ref):
    ang = pos_ref[...] * freq_ref[...]
    cos_ref[...] = jnp.cos(ang)
    sin_ref[...] = jnp.sin(ang) * sign_ref[...]


def rope_tables(pos_f, freq, sign):
    b, s, _ = pos_f.shape
    out = jax.ShapeDtypeStruct((b, s, LANES), F32)
    return pl.pallas_call(
        _rope_table_kernel,
        grid=(b,),
        in_specs=[pl.BlockSpec((None, s, 1), lambda i: (i, 0, 0)),
                  pl.BlockSpec((1, LANES), lambda i: (0, 0)),
                  pl.BlockSpec((1, LANES), lambda i: (0, 0))],
        out_specs=[pl.BlockSpec((None, s, LANES), lambda i: (i, 0, 0))] * 2,
        out_shape=[out, out],
        compiler_params=_params(("parallel",)),
        name="rope_tables",
    )(pos_f, freq, sign)


def _retention_kernel(lg_ref, q_ref, k_ref, v_ref, g_ref, gain_ref, o_ref, state_ref, *, seq):
    h = pl.program_id(1)
    lg = lg_ref[h]
    c = RET_CHUNK
    row = lax.broadcasted_iota(jnp.int32, (c, c), 0)
    col = lax.broadcasted_iota(jnp.int32, (c, c), 1)
    diff = (row - col).astype(F32)
    k_scale = RET_HEAD_DIM ** -0.5
    intra = jnp.where(diff >= 0, jnp.exp(jnp.maximum(diff, 0.0) * lg), 0.0) * k_scale
    idx = lax.broadcasted_iota(jnp.int32, (c, 1), 0).astype(F32)
    q_dec = jnp.exp((idx + 1.0) * lg)
    k_dec = jnp.exp((c - 1.0 - idx) * lg) * k_scale
    chunk_dec = jnp.exp(jnp.full((1, 1), float(c), F32) * lg)
    gain = gain_ref[...]
    state_ref[...] = jnp.zeros_like(state_ref)

    def body(n, carry):
        rows = pl.ds(pl.multiple_of(n * c, c), c)
        qb, kb = q_ref[rows, :], k_ref[rows, :]
        qc, kc = qb.astype(F32), kb.astype(F32)
        vc = v_ref[rows, :]
        scores = _dot_nt(qb, kb) * intra
        o = _dot(scores.astype(BF16), vc)
        state = state_ref[...]
        o = o + _dot((qc * q_dec).astype(BF16), state.astype(BF16))
        state_ref[...] = state * chunk_dec + _dot_tn((kc * k_dec).astype(BF16), vc)
        ms = jnp.mean(o * o, axis=-1, keepdims=True)
        y = o * lax.rsqrt(ms + EPS) * gain
        o_ref[rows, :] = (_silu(g_ref[rows, :].astype(F32)) * y).astype(o_ref.dtype)
        return carry

    lax.fori_loop(0, seq // c, body, 0, unroll=True)


def retention_mixer(proj, log_g, gain, batch, seq):
    hd = RET_HEAD_DIM
    nh = RET_HEADS

    def col_spec(offset):
        return pl.BlockSpec((seq, hd), lambda b, h, lg: (b, offset + h))

    grid_spec = pltpu.PrefetchScalarGridSpec(
        num_scalar_prefetch=1,
        grid=(batch, nh),
        in_specs=[col_spec(0), col_spec(nh), col_spec(2 * nh), col_spec(3 * nh),
                  pl.BlockSpec((None, 1, hd), lambda b, h, lg: (h, 0, 0))],
        out_specs=pl.BlockSpec((seq, hd), lambda b, h, lg: (b, h)),
        scratch_shapes=[pltpu.VMEM((hd, hd), F32)],
    )
    return pl.pallas_call(
        functools.partial(_retention_kernel, seq=seq),
        grid_spec=grid_spec,
        out_shape=jax.ShapeDtypeStruct((batch * seq, TOKEN_WIDTH), BF16),
        compiler_params=_params(("parallel", "parallel")),
        name="retention",
    )(log_g, proj, proj, proj, proj, gain.reshape(nh, 1, hd))


SWA_BLOCKS_PER_STEP = 2


def _swa_kernel(sink_ref, q_ref, kc_ref, kp_ref, vc_ref, vp_ref, o_ref):
    n = pl.program_id(1)
    w = WINDOW
    dh = SWA_HEAD_DIM
    nsub = SWA_BLOCKS_PER_STEP
    lane = lax.broadcasted_iota(jnp.int32, (1, LANES), 1)
    low_half = lane < dh

    i_idx = lax.broadcasted_iota(jnp.int32, (w, 2 * w), 0)
    j_idx = lax.broadcasted_iota(jnp.int32, (w, 2 * w), 1)
    band = (j_idx > i_idx) & (j_idx <= i_idx + w)
    neg = jnp.finfo(F32).min
    sink_col = lane == 0
    first_key = lax.broadcasted_iota(jnp.int32, (2 * w, 1), 0) == 0
    e_row = lax.broadcasted_iota(jnp.int32, (4 * w, LANES), 0) < 2 * w
    e_lane = lax.broadcasted_iota(jnp.int32, (4 * w, LANES), 1) < dh
    e_mat = (e_row == e_lane).astype(BF16)

    n_kv_slabs = SWA_KV_HEADS * dh // LANES
    k_slabs, v_slabs = [], []
    for s in range(n_kv_slabs):
        cols = slice(s * LANES, (s + 1) * LANES)
        k_slabs.append(jnp.concatenate([kp_ref[:, cols], kc_ref[:, cols]], axis=0).astype(F32))
        v_slabs.append(jnp.concatenate([vp_ref[:, cols], vc_ref[:, cols]], axis=0))

    n_pair = SWA_GROUP // 2
    zero = jnp.zeros((w, LANES), BF16)
    for sub in range(nsub):
        q_rows_of_step = slice(sub * w, (sub + 1) * w)
        kv_rows = slice(sub * w, (sub + 2) * w)
        valid = band if sub > 0 else band & ((n > 0) | (j_idx >= w))
        bias = jnp.concatenate([jnp.where(valid, 0.0, neg)] * SWA_GROUP, axis=0)
        for kv in range(SWA_KV_HEADS):
            k_slab = k_slabs[kv // 2][kv_rows]
            v_slab = v_slabs[kv // 2][kv_rows]
            v_slab = jnp.where(first_key, jnp.zeros_like(v_slab), v_slab)
            if kv % 2 == 0:
                k_own = jnp.where(low_half, k_slab, 0.0)
                v_lo = jnp.where(low_half, v_slab, jnp.zeros_like(v_slab))
                v_hi = pltpu.roll(v_lo.astype(F32), dh, 1).astype(BF16)
            else:
                k_own = jnp.where(low_half, 0.0, k_slab)
                v_hi = jnp.where(low_half, jnp.zeros_like(v_slab), v_slab)
                v_lo = pltpu.roll(v_hi.astype(F32), dh, 1).astype(BF16)
            kk = (k_own + pltpu.roll(k_own, dh, 1)).astype(BF16)
            v_ext = jnp.concatenate([jnp.concatenate([v_lo, v_hi], axis=0), e_mat], axis=1)

            q_rows = []
            for pair in range(n_pair):
                col0 = (kv * n_pair + pair) * LANES
                qn = q_ref[q_rows_of_step, col0:col0 + LANES]
                q_rows += [jnp.where(low_half, qn, zero), jnp.where(low_half, zero, qn)]
            q8 = jnp.concatenate(q_rows, axis=0)
            s = _dot_nt(q8, kk) + bias
            s_first = jnp.concatenate(
                [jnp.where(sink_col, sink_ref[kv * SWA_GROUP + g], s[g * w:(g + 1) * w, :LANES])
                 for g in range(SWA_GROUP)], axis=0)
            s = jnp.concatenate([s_first, s[:, LANES:]], axis=1)
            p = jnp.exp(s - jnp.max(s, axis=-1, keepdims=True)).astype(BF16)
            p_pairs = jnp.concatenate(
                [jnp.concatenate([p[(2 * i) * w:(2 * i + 1) * w], p[(2 * i + 1) * w:(2 * i + 2) * w]], axis=1)
                 for i in range(n_pair)], axis=0)
            ov = _dot(p_pairs, v_ext)
            out = ov[:, :LANES] / ov[:, LANES:]
            for pair in range(n_pair):
                col0 = (kv * n_pair + pair) * LANES
                o_ref[q_rows_of_step, col0:col0 + LANES] = out[pair * w:(pair + 1) * w].astype(o_ref.dtype)


def swa_mixer(proj, sinks, batch, seq):
    w = WINDOW
    rows = SWA_BLOCKS_PER_STEP * w
    ns = seq // rows
    kv_w = SWA_KV_HEADS * SWA_HEAD_DIM
    k_blk = TOKEN_WIDTH // kv_w

    def cur(b, n, s):
        return b * ns + n

    def prev(b, n, s):
        return jnp.maximum((b * ns + n) * SWA_BLOCKS_PER_STEP - 1, b * ns * SWA_BLOCKS_PER_STEP)

    grid_spec = pltpu.PrefetchScalarGridSpec(
        num_scalar_prefetch=1,
        grid=(batch, ns),
        in_specs=[pl.BlockSpec((rows, TOKEN_WIDTH), lambda b, n, s: (cur(b, n, s), 0)),
                  pl.BlockSpec((rows, kv_w), lambda b, n, s: (cur(b, n, s), k_blk)),
                  pl.BlockSpec((w, kv_w), lambda b, n, s: (prev(b, n, s), k_blk)),
                  pl.BlockSpec((rows, kv_w), lambda b, n, s: (cur(b, n, s), k_blk + 1)),
                  pl.BlockSpec((w, kv_w), lambda b, n, s: (prev(b, n, s), k_blk + 1))],
        out_specs=pl.BlockSpec((rows, TOKEN_WIDTH), lambda b, n, s: (cur(b, n, s), 0)),
    )
    return pl.pallas_call(
        _swa_kernel,
        grid_spec=grid_spec,
        out_shape=jax.ShapeDtypeStruct((batch * seq, TOKEN_WIDTH), BF16),
        compiler_params=_params(("parallel", "parallel")),
        name="swa",
    )(sinks, proj, proj, proj, proj, proj)


def _mem_attn_kernel(*refs):
    q_refs = refs[:MEM_HEADS]
    mk_ref, mv_ref, qg_ref, kg_ref, w_ref, o_ref, w16_ref = refs[MEM_HEADS:]
    hd = MEM_HEAD_DIM
    w16_ref[...] = w_ref[...].astype(w16_ref.dtype)

    def norm(x, g):
        ms = jnp.mean(x * x, axis=-1, keepdims=True)
        return (x * lax.rsqrt(ms + EPS) * g).astype(BF16)

    for h in range(MEM_HEADS):
        cols = slice(h * hd, (h + 1) * hd)
        q = norm(q_refs[h][...].astype(F32), qg_ref[...])
        k = norm(mk_ref[:, cols], kg_ref[...])
        s = _dot_nt(q, k) * (hd ** -0.5)
        p = jnp.exp(s - jnp.max(s, axis=-1, keepdims=True))
        p = (p / jnp.sum(p, axis=-1, keepdims=True)).astype(BF16)
        o_ref[:, cols] = _dot(p, mv_ref[:, cols].astype(BF16)).astype(o_ref.dtype)


def memory_attention(proj, q_col_block, mkv, q_gain, k_gain, w_o, batch, seq, block_rows=1024):
    hd = MEM_HEAD_DIM
    nblk = seq // block_rows
    w_rows = w_o.shape[0] // (batch * nblk)
    w_spec = pl.BlockSpec((w_rows, w_o.shape[1]), lambda b, i: (b * nblk + i, 0))
    gain_spec = pl.BlockSpec((1, hd), lambda b, i: (0, 0))
    q_specs = [pl.BlockSpec((block_rows, hd),
                            functools.partial(lambda b, i, h: (b * nblk + i, q_col_block + h), h=h))
               for h in range(MEM_HEADS)]
    return pl.pallas_call(
        _mem_attn_kernel,
        grid=(batch, nblk),
        in_specs=q_specs + [pl.BlockSpec((N_MEM, MEM_WIDTH), lambda b, i: (b, 0)),
                            pl.BlockSpec((N_MEM, MEM_WIDTH), lambda b, i: (b, 1)),
                            gain_spec, gain_spec, w_spec],
        out_specs=[pl.BlockSpec((block_rows, MEM_WIDTH), lambda b, i: (b * nblk + i, 0)), w_spec],
        out_shape=[jax.ShapeDtypeStruct((batch * seq, MEM_WIDTH), BF16),
                   jax.ShapeDtypeStruct(w_o.shape, BF16)],
        compiler_params=_params(("parallel", "parallel")),
        name="mem_attn",
    )(*([proj] * MEM_HEADS), mkv, mkv, q_gain.reshape(1, hd), k_gain.reshape(1, hd), w_o)


def _norm_operands(x_new, gain, o_ref, xg_ref, rstd_ref, ssq_ref):
    o_ref[...] = x_new
    xg_ref[...] = (x_new * gain).astype(xg_ref.dtype)
    ssq = ssq_ref[...] + jnp.sum(x_new * x_new, axis=-1, keepdims=True)
    ssq_ref[...] = ssq
    rstd_ref[...] = lax.rsqrt(ssq * (1.0 / D_MODEL) + EPS)


def _out_proj_kernel(tok_ref, mem_ref, w_ref, x_ref, g_ref, o_ref, xg_ref, rstd_ref, ssq_ref):
    @pl.when(pl.program_id(1) == 0)
    def _():
        ssq_ref[...] = jnp.zeros_like(ssq_ref)

    acc = _dot(tok_ref[...], w_ref[:TOKEN_WIDTH, :])
    acc = acc + _dot(mem_ref[...], w_ref[TOKEN_WIDTH:, :])
    _norm_operands(x_ref[...] + acc, g_ref[...], o_ref, xg_ref, rstd_ref, ssq_ref)


def out_projection(tok, mem, w_o, x, next_gain):
    t = x.shape[0]
    bm, bn = 1024, 512
    return pl.pallas_call(
        _out_proj_kernel,
        grid=(t // bm, D_MODEL // bn),
        in_specs=[pl.BlockSpec((bm, TOKEN_WIDTH), lambda i, j: (i, 0)),
                  pl.BlockSpec((bm, MEM_WIDTH), lambda i, j: (i, 0)),
                  pl.BlockSpec((D_MODEL, bn), lambda i, j: (0, j)),
                  pl.BlockSpec((bm, bn), lambda i, j: (i, j)),
                  pl.BlockSpec((1, bn), lambda i, j: (0, j))],
        out_specs=[pl.BlockSpec((bm, bn), lambda i, j: (i, j)),
                   pl.BlockSpec((bm, bn), lambda i, j: (i, j)),
                   pl.BlockSpec((bm, 1), lambda i, j: (i, 0))],
        out_shape=[jax.ShapeDtypeStruct((t, D_MODEL), F32),
                   jax.ShapeDtypeStruct((t, D_MODEL), BF16),
                   jax.ShapeDtypeStruct((t, 1), F32)],
        scratch_shapes=[pltpu.VMEM((bm, 1), F32)],
        compiler_params=_params(("parallel", "arbitrary")),
        name="out_proj",
    )(tok, mem, w_o, x, next_gain.reshape(1, D_MODEL))


FFN_BM = 1024
FFN_RM = 512
FFN_FC = 256
FFN_NC = 256
FFN_VMEM_LIMIT = 62 * 1024 * 1024


def _ffn_kernel(xg_ref, rstd_ref, wg_ref, wu_ref, wd_ref, x_ref, *rest, n_f, emit_norm):
    if emit_norm:
        g_ref, o_ref, xg_out_ref, rstd_out_ref, acc_ref, hid_ref, ssq_ref = rest
    else:
        o_ref, acc_ref, hid_ref = rest
    f = pl.program_id(1)
    n_slab = D_MODEL // FFN_NC

    @pl.when(f == 0)
    def _():
        acc_ref[...] = jnp.zeros_like(acc_ref)
        if emit_norm:
            ssq_ref[...] = jnp.zeros_like(ssq_ref)

    @pl.when(f < n_slab)
    def _():
        acc_ref[f] += x_ref[...]

    @pl.when(f < n_f)
    def _():
        wg = wg_ref[...].astype(BF16)
        wu = wu_ref[...].astype(BF16)
        for r in range(FFN_BM // FFN_RM):
            rows = slice(r * FFN_RM, (r + 1) * FFN_RM)
            xg = xg_ref[rows, :]
            rstd = rstd_ref[rows, :]
            gate = _dot(xg, wg) * rstd
            up = _dot(xg, wu) * rstd
            hid_ref[rows, :] = (_silu(gate) * up).astype(BF16)
        hid = hid_ref[...]
        for c in range(n_slab):
            acc_ref[c] += _dot(hid, wd_ref[:, c * FFN_NC:(c + 1) * FFN_NC].astype(BF16))

    @pl.when(f >= n_f)
    def _():
        x_new = acc_ref[f - n_f]
        if emit_norm:
            _norm_operands(x_new, g_ref[...], o_ref, xg_out_ref, rstd_out_ref, ssq_ref)
        else:
            o_ref[...] = x_new


def ffn(xg, rstd, w_gate_up, w_down, x, next_gain=None):
    t = x.shape[0]
    n_f = D_FF // FFN_FC
    n_slab = D_MODEL // FFN_NC
    last = n_f - 1
    emit_norm = next_gain is not None

    def fidx(f):
        return jnp.minimum(f, last)

    def oidx(f):
        return jnp.maximum(f - n_f, 0)

    slab_spec = pl.BlockSpec((FFN_BM, FFN_NC), lambda i, f: (i, oidx(f)))
    x_spec = pl.BlockSpec((FFN_BM, FFN_NC), lambda i, f: (i, jnp.minimum(f, n_slab - 1)))
    in_specs = [pl.BlockSpec((FFN_BM, D_MODEL), lambda i, f: (i, 0), pipeline_mode=pl.Buffered(1)),
                pl.BlockSpec((FFN_BM, 1), lambda i, f: (i, 0)),
                pl.BlockSpec((D_MODEL, FFN_FC), lambda i, f: (0, fidx(f))),
                pl.BlockSpec((D_MODEL, FFN_FC), lambda i, f: (0, n_f + fidx(f))),
                pl.BlockSpec((FFN_FC, D_MODEL), lambda i, f: (fidx(f), 0)),
                x_spec]
    args = [xg, rstd, w_gate_up, w_gate_up, w_down, x]
    out_specs = [slab_spec]
    out_shape = [jax.ShapeDtypeStruct((t, D_MODEL), F32)]
    scratch = [pltpu.VMEM((n_slab, FFN_BM, FFN_NC), F32), pltpu.VMEM((FFN_BM, FFN_FC), BF16)]
    if emit_norm:
        in_specs.append(pl.BlockSpec((1, FFN_NC), lambda i, f: (0, oidx(f))))
        args.append(next_gain.reshape(1, D_MODEL))
        out_specs += [slab_spec, pl.BlockSpec((FFN_BM, 1), lambda i, f: (i, 0))]
        out_shape += [jax.ShapeDtypeStruct((t, D_MODEL), BF16), jax.ShapeDtypeStruct((t, 1), F32)]
        scratch.append(pltpu.VMEM((FFN_BM, 1), F32))
    out = pl.pallas_call(
        functools.partial(_ffn_kernel, n_f=n_f, emit_norm=emit_norm),
        grid=(t // FFN_BM, n_f + n_slab),
        in_specs=in_specs,
        out_specs=out_specs,
        out_shape=out_shape,
        scratch_shapes=scratch,
        compiler_params=_params(("parallel", "arbitrary"), FFN_VMEM_LIMIT),
        name="ffn",
    )(*args)
    return out if emit_norm else out[0]


def _rope_consts(head_dim):
    half = head_dim // 2
    inv_freq = ROPE_THETA ** (-jnp.arange(0, head_dim, 2, dtype=F32) / head_dim)
    reps = LANES // half if half < LANES else 1
    freq = jnp.tile(inv_freq, reps)[:LANES]
    if half >= LANES:
        sign = jnp.ones((LANES,), F32)
    else:
        sign = jnp.tile(jnp.concatenate([-jnp.ones((half,), F32), jnp.ones((half,), F32)]), reps // 2)
    return freq.reshape(1, LANES), sign.reshape(1, LANES)


def kernel(x, mem, positions, mem_norm_g, w_mem_kv, mem_k_norm_g, l0_attn_norm_g, l0_w_in, l0_ret_norm_g, l0_mem_q_norm_g, l0_w_o, l0_ffn_norm_g, l0_w_gate_up, l0_w_down, l1_attn_norm_g, l1_w_in, l1_q_norm_g, l1_k_norm_g, l1_sinks, l1_mem_q_norm_g, l1_w_o, l1_ffn_norm_g, l1_w_gate_up, l1_w_down):
    batch, seq, d = x.shape
    t = batch * seq
    x2 = x.reshape(t, d)

    pos_f = positions.astype(F32).reshape(batch, seq, 1)
    cos_r, sin_r = rope_tables(pos_f, *_rope_consts(RET_HEAD_DIM))
    cos_s, sin_s = rope_tables(pos_f, *_rope_consts(SWA_HEAD_DIM))
    log_g = jnp.log1p(-jnp.exp2(-5.0 - jnp.arange(RET_HEADS, dtype=F32)))

    mem_n = rmsnorm_rows(mem.reshape(batch * N_MEM, d), mem_norm_g)
    mkv = matmul(mem_n, w_mem_kv, F32)

    h = rmsnorm_rows(x2, l0_attn_norm_g)
    proj = retention_projection(h, l0_w_in, cos_r.reshape(t, LANES), sin_r.reshape(t, LANES))
    tok = retention_mixer(proj, log_g, l0_ret_norm_g, batch, seq)
    mo, w_o = memory_attention(proj, 4 * TOKEN_WIDTH // MEM_HEAD_DIM, mkv, l0_mem_q_norm_g, mem_k_norm_g,
                               l0_w_o, batch, seq)
    x2, xg, rstd = out_projection(tok, mo, w_o, x2, l0_ffn_norm_g)
    x2, xg, rstd = ffn(xg, rstd, l0_w_gate_up, l0_w_down, x2, l1_attn_norm_g)

    proj = swa_projection(xg, rstd, l1_w_in, cos_s.reshape(t, LANES), sin_s.reshape(t, LANES),
                          l1_q_norm_g, l1_k_norm_g)
    tok = swa_mixer(proj, l1_sinks, batch, seq)
    q_col = (TOKEN_WIDTH + 2 * SWA_KV_HEADS * SWA_HEAD_DIM) // MEM_HEAD_DIM
    mo, w_o = memory_attention(proj, q_col, mkv, l1_mem_q_norm_g, mem_k_norm_g, l1_w_o, batch, seq)
    x2, xg, rstd = out_projection(tok, mo, w_o, x2, l1_ffn_norm_g)
    x2 = ffn(xg, rstd, l1_w_gate_up, l1_w_down, x2)
    return x2.reshape(batch, seq, d)
```
